```python
import jax, jax.numpy as jnp
from jax import lax
import numpy as np

D_MODEL = 4096
BATCH = 2
SEQ = 4096
DEPTH = 1
DEC_BATCH = 32
DEC_SEQ = 1
PAST_LEN = 8192
PAGE_SIZE = 128

HEAD_DIM = 128
FOX_HEADS = 16
FOX_WIDTH = FOX_HEADS * HEAD_DIM
CONV_WIDTH = D_MODEL // 2
CONV_K = 3
MEM_TOKENS = 256
MEM_HEADS = 4
MEM_WIDTH = MEM_HEADS * HEAD_DIM
N_BRANCH = 3
D_FF = 11008
Q_BLOCK = 128
FORGET_BIAS_MEAN = 4.0
POOL_NUM = 5
POOL_DEN = 4
EPS = 1e-6
ATTN_SCALE = HEAD_DIM ** -0.5
N_PAGES = PAST_LEN // PAGE_SIZE
IN_SPLITS = (FOX_WIDTH, FOX_WIDTH, FOX_WIDTH, FOX_HEADS, CONV_WIDTH, CONV_WIDTH, CONV_WIDTH, MEM_WIDTH, N_BRANCH * D_MODEL)
D_IN = 3 * FOX_WIDTH + FOX_HEADS + 3 * CONV_WIDTH + MEM_WIDTH + N_BRANCH * D_MODEL

kernel_name = "fox_shortconv_memory_hybrid_step"


def rmsnorm(x, g):
    xf = x.astype(jnp.float32)
    y = xf * lax.rsqrt(jnp.mean(xf * xf, axis=-1, keepdims=True) + EPS)
    return (y * g.astype(jnp.float32)).astype(x.dtype)


def split_heads(t, n):
    return t.reshape(*t.shape[:-1], n, HEAD_DIM)


def project_in(x, g_mix, w_in, b_f, q_norm_g, k_norm_g, mq_norm_g):
    h = rmsnorm(x, g_mix)
    z = h @ w_in
    offs = [int(o) for o in np.cumsum(IN_SPLITS)[:-1]]
    q, k, v, f_logit, c_b, c_c, c_x, mq, gates = jnp.split(z, offs, axis=-1)
    q = rmsnorm(split_heads(q, FOX_HEADS), q_norm_g)
    k = rmsnorm(split_heads(k, FOX_HEADS), k_norm_g)
    v = split_heads(v, FOX_HEADS)
    logf = jax.nn.log_sigmoid(f_logit.astype(jnp.float32) + b_f.astype(jnp.float32))
    mq = rmsnorm(split_heads(mq, MEM_HEADS), mq_norm_g)
    u = c_c * c_x
    return q, k, v, logf, c_b, u, mq, gates


def causal_dwconv(u, hist, w):
    T = u.shape[1]
    ext = jnp.concatenate([hist.astype(u.dtype), u], axis=1)
    y = ext[:, 0:T] * w[0]
    for i in range(1, CONV_K):
        y = y + ext[:, i:i + T] * w[i]
    return y, ext[:, T:]


def fox_prompt(q, k, v, logf):
    B, S, H, Dh = q.shape
    nb = S // Q_BLOCK
    c = jnp.cumsum(logf, axis=1)
    c_k = c.transpose(0, 2, 1)[:, :, None, :]
    kpos = jnp.arange(S)

    def block(args):
        q_blk, c_blk, start = args
        s = jnp.einsum('bqhd,bkhd->bhqk', q_blk, k).astype(jnp.float32) * ATTN_SCALE
        s = s + c_blk.transpose(0, 2, 1)[..., None] - c_k
        qpos = start + jnp.arange(Q_BLOCK)
        s = jnp.where(kpos[None, :] <= qpos[:, None], s, -jnp.inf)
        p = jax.nn.softmax(s, axis=-1)
        return jnp.einsum('bhqk,bkhd->bqhd', p.astype(v.dtype), v)

    qb = q.reshape(B, nb, Q_BLOCK, H, Dh).transpose(1, 0, 2, 3, 4)
    cb = c.reshape(B, nb, Q_BLOCK, H).transpose(1, 0, 2, 3)
    starts = jnp.arange(nb) * Q_BLOCK
    o = lax.map(block, (qb, cb, starts))
    return o.transpose(1, 0, 2, 3, 4).reshape(B, S, H * Dh)


def fox_sample(q, k_new, v_new, logf_new, k_past, v_past, logf_past):
    DB, T, H, Dh = q.shape
    P = k_past.shape[1]
    c_past = jnp.cumsum(logf_past.astype(jnp.float32), axis=1)
    c_new = c_past[:, -1:] + jnp.cumsum(logf_new, axis=1)
    cq = c_new.transpose(0, 2, 1)[..., None]
    s_past = jnp.einsum('bqhd,bkhd->bhqk', q, k_past).astype(jnp.float32) * ATTN_SCALE
    s_past = s_past + cq - c_past.transpose(0, 2, 1)[:, :, None, :]
    s_new = jnp.einsum('bqhd,bkhd->bhqk', q, k_new).astype(jnp.float32) * ATTN_SCALE
    s_new = s_new + cq - c_new.transpose(0, 2, 1)[:, :, None, :]
    tri = jnp.tril(jnp.ones((T, T), dtype=bool))
    s_new = jnp.where(tri, s_new, -jnp.inf)
    p = jax.nn.softmax(jnp.concatenate([s_past, s_new], axis=-1), axis=-1)
    o = (jnp.einsum('bhqk,bkhd->bqhd', p[..., :P].astype(v_past.dtype), v_past)
         + jnp.einsum('bhqk,bkhd->bqhd', p[..., P:].astype(v_new.dtype), v_new))
    return o.reshape(DB, T, H * Dh)


def mem_kv(mem, g_mem, w_mem_kv, mk_norm_g):
    hm = rmsnorm(mem, g_mem)
    mk, mv = jnp.split(hm @ w_mem_kv, 2, axis=-1)
    return rmsnorm(split_heads(mk, MEM_HEADS), mk_norm_g), split_heads(mv, MEM_HEADS)


def mem_attend(mq, mk, mv):
    B, T = mq.shape[0], mq.shape[1]
    s = jnp.einsum('bqhd,bmhd->bhqm', mq, mk).astype(jnp.float32) * ATTN_SCALE
    p = jax.nn.softmax(s, axis=-1)
    o = jnp.einsum('bhqm,bmhd->bqhd', p.astype(mv.dtype), mv)
    return o.reshape(B, T, MEM_WIDTH)


def merge_out(x, a, b, m, gates, w_o_fox, w_o_conv, w_o_mem, w_o):
    g = jax.nn.sigmoid(gates.astype(jnp.float32)).astype(x.dtype)
    g_a, g_b, g_m = jnp.split(g, N_BRANCH, axis=-1)
    merged = g_a * (a @ w_o_fox) + g_b * (b @ w_o_conv) + g_m * (m @ w_o_mem)
    return x + merged @ w_o


def conv_ffn(x, hist, g_ffn, w_up, ffn_conv_w, w_down):
    h = rmsnorm(x, g_ffn)
    gate, val = jnp.split(h @ w_up, 2, axis=-1)
    gate_c, new_hist = causal_dwconv(gate, hist, ffn_conv_w)
    return x + (jax.nn.silu(gate_c) * val) @ w_down, new_hist


def setup_inputs(seed: int = 0) -> dict:
    key = jax.random.key(seed)
    ks = jax.random.split(key, 32)
    n_pool = (DEC_BATCH * N_PAGES * POOL_NUM) // POOL_DEN
    nrm = lambda k, shape: jax.random.normal(k, shape, jnp.float32)
    w = lambda k, shape, fan_in: nrm(k, (DEPTH,) + shape) * fan_in ** -0.5
    gain = lambda k, shape: 1.0 + 0.02 * nrm(k, (DEPTH,) + shape)
    page_table = jax.random.permutation(ks[10], n_pool)[:DEC_BATCH * N_PAGES].reshape(DEC_BATCH, N_PAGES).astype(jnp.int32)
    return {
        "x_prompt": nrm(ks[0], (BATCH, SEQ, D_MODEL)),
        "mem_prompt": nrm(ks[1], (BATCH, MEM_TOKENS, D_MODEL)),
        "x_sample": nrm(ks[2], (DEC_BATCH, DEC_SEQ, D_MODEL)),
        "cache_k": nrm(ks[3], (DEPTH, n_pool, PAGE_SIZE, FOX_HEADS, HEAD_DIM)),
        "cache_v": nrm(ks[4], (DEPTH, n_pool, PAGE_SIZE, FOX_HEADS, HEAD_DIM)),
        "cache_logf": jax.nn.log_sigmoid(FORGET_BIAS_MEAN + nrm(ks[5], (DEPTH, n_pool, PAGE_SIZE, FOX_HEADS))),
        "cache_mem_k": nrm(ks[6], (DEPTH, DEC_BATCH, MEM_TOKENS, MEM_HEADS, HEAD_DIM)),
        "cache_mem_v": nrm(ks[7], (DEPTH, DEC_BATCH, MEM_TOKENS, MEM_HEADS, HEAD_DIM)),
        "state_conv": nrm(ks[8], (DEPTH, DEC_BATCH, CONV_K - 1, CONV_WIDTH)),
        "state_ffn_conv": nrm(ks[9], (DEPTH, DEC_BATCH, CONV_K - 1, D_FF)),
        "page_table": page_table,
        "g_mix": gain(ks[11], (D_MODEL,)),
        "w_in": w(ks[12], (D_MODEL, D_IN), D_MODEL),
        "b_f": FORGET_BIAS_MEAN + 0.1 * nrm(ks[13], (DEPTH, FOX_HEADS)),
        "q_norm_g": gain(ks[14], (HEAD_DIM,)),
        "k_norm_g": gain(ks[15], (HEAD_DIM,)),
        "mq_norm_g": gain(ks[16], (HEAD_DIM,)),
        "mk_norm_g": gain(ks[17], (HEAD_DIM,)),
        "g_mem": gain(ks[18], (D_MODEL,)),
        "w_mem_kv": w(ks[19], (D_MODEL, 2 * MEM_WIDTH), D_MODEL),
        "conv_w": w(ks[20], (CONV_K, CONV_WIDTH), CONV_K),
        "w_o_fox": w(ks[21], (FOX_WIDTH, D_MODEL), FOX_WIDTH),
        "w_o_conv": w(ks[22], (CONV_WIDTH, D_MODEL), CONV_WIDTH),
        "w_o_mem": w(ks[23], (MEM_WIDTH, D_MODEL), MEM_WIDTH),
        "w_o": w(ks[24], (D_MODEL, D_MODEL), D_MODEL),
        "g_ffn": gain(ks[25], (D_MODEL,)),
        "w_up": w(ks[26], (D_MODEL, 2 * D_FF), D_MODEL),
        "ffn_conv_w": w(ks[27], (CONV_K, D_FF), CONV_K),
        "w_down": w(ks[28], (D_FF, D_MODEL), D_FF),
    }


def reference(x_prompt, mem_prompt, x_sample, cache_k, cache_v, cache_logf, cache_mem_k, cache_mem_v,
              state_conv, state_ffn_conv, page_table, g_mix, w_in, b_f, q_norm_g, k_norm_g, mq_norm_g,
              mk_norm_g, g_mem, w_mem_kv, conv_w, w_o_fox, w_o_conv, w_o_mem, w_o, g_ffn, w_up,
              ffn_conv_w, w_down):
    xp, xs = x_prompt, x_sample
    kp_l, vp_l, fp_l, mkp_l, mvp_l, cp_l, ffp_l = [], [], [], [], [], [], []
    ks_l, vs_l, fs_l, cs_l, ffs_l = [], [], [], [], []
    n_seq_pages = SEQ // PAGE_SIZE
    for l in range(DEPTH):
        q, k, v, logf, c_b, u, mq, gates = project_in(xp, g_mix[l], w_in[l], b_f[l], q_norm_g[l], k_norm_g[l], mq_norm_g[l])
        a = fox_prompt(q, k, v, logf)
        u_c, conv_hist = causal_dwconv(u, jnp.zeros((BATCH, CONV_K - 1, CONV_WIDTH), u.dtype), conv_w[l])
        mk, mv = mem_kv(mem_prompt, g_mem[l], w_mem_kv[l], mk_norm_g[l])
        m = mem_attend(mq, mk, mv)
        xp = merge_out(xp, a, c_b * u_c, m, gates, w_o_fox[l], w_o_conv[l], w_o_mem[l], w_o[l])
        xp, ffn_hist = conv_ffn(xp, jnp.zeros((BATCH, CONV_K - 1, D_FF), xp.dtype), g_ffn[l], w_up[l], ffn_conv_w[l], w_down[l])
        kp_l.append(k.reshape(BATCH, n_seq_pages, PAGE_SIZE, FOX_HEADS, HEAD_DIM))
        vp_l.append(v.reshape(BATCH, n_seq_pages, PAGE_SIZE, FOX_HEADS, HEAD_DIM))
        fp_l.append(logf.reshape(BATCH, n_seq_pages, PAGE_SIZE, FOX_HEADS))
        mkp_l.append(mk)
        mvp_l.append(mv)
        cp_l.append(conv_hist)
        ffp_l.append(ffn_hist)
        q, k, v, logf, c_b, u, mq, gates = project_in(xs, g_mix[l], w_in[l], b_f[l], q_norm_g[l], k_norm_g[l], mq_norm_g[l])
        k_past = cache_k[l, page_table].reshape(DEC_BATCH, N_PAGES * PAGE_SIZE, FOX_HEADS, HEAD_DIM)
        v_past = cache_v[l, page_table].reshape(DEC_BATCH, N_PAGES * PAGE_SIZE, FOX_HEADS, HEAD_DIM)
        f_past = cache_logf[l, page_table].reshape(DEC_BATCH, N_PAGES * PAGE_SIZE, FOX_HEADS)
        a = fox_sample(q, k, v, logf, k_past, v_past, f_past)
        u_c, conv_hist = causal_dwconv(u, state_conv[l], conv_w[l])
        m = mem_attend(mq, cache_mem_k[l], cache_mem_v[l])
        xs = merge_out(xs, a, c_b * u_c, m, gates, w_o_fox[l], w_o_conv[l], w_o_mem[l], w_o[l])
        xs, ffn_hist = conv_ffn(xs, state_ffn_conv[l], g_ffn[l], w_up[l], ffn_conv_w[l], w_down[l])
        ks_l.append(k)
        vs_l.append(v)
        fs_l.append(logf)
        cs_l.append(conv_hist)
        ffs_l.append(ffn_hist)
    return (xp, xs,
            jnp.stack(kp_l), jnp.stack(vp_l), jnp.stack(fp_l), jnp.stack(mkp_l), jnp.stack(mvp_l),
            jnp.stack(cp_l), jnp.stack(ffp_l),
            jnp.stack(ks_l), jnp.stack(vs_l), jnp.stack(fs_l), jnp.stack(cs_l), jnp.stack(ffs_l))
```

```python
import functools

import jax
import jax.numpy as jnp
from jax import lax
from jax.experimental import pallas as pl
from jax.experimental.pallas import tpu as pltpu

F32 = jnp.float32
BF16 = jnp.bfloat16

EPS = 1e-6
HEAD_DIM = 128
LANES = 128
BF16_SUBLANES = 16
FOX_HEADS = 16
MEM_HEADS = 4
MEM_TOKENS = 256
PAGE = 128
ATTN_SCALE = HEAD_DIM ** -0.5
VMEM_LIMIT = 56 * 1024 * 1024


def _params(n_axes):
    return pltpu.CompilerParams(dimension_semantics=("arbitrary",) * n_axes,
                                vmem_limit_bytes=VMEM_LIMIT)


def _rmsnorm_kernel(x_ref, g_ref, o_ref):
    x = x_ref[...]
    y = x * lax.rsqrt(jnp.mean(x * x, axis=-1, keepdims=True) + EPS)
    o_ref[...] = (y * g_ref[...]).astype(o_ref.dtype)


def rmsnorm_bf16(x, g, *, tr=256):
    m, d = x.shape
    tr = min(tr, m)
    return pl.pallas_call(
        _rmsnorm_kernel,
        grid=(m // tr,),
        in_specs=[pl.BlockSpec((tr, d), lambda i: (i, 0)),
                  pl.BlockSpec((1, d), lambda i: (0, 0))],
        out_specs=pl.BlockSpec((tr, d), lambda i: (i, 0)),
        out_shape=jax.ShapeDtypeStruct((m, d), BF16),
        compiler_params=_params(1),
        name="rmsnorm",
    )(x, g.reshape(1, d))


def _fused_mm_kernel(*refs, n_act, pair_act, n_rows, n_tiles, n_out, halo, tiles_per_seq, epilogue):
    pos = 0
    act_refs = refs[pos:pos + n_act]; pos += n_act
    halo_ref = None
    if halo:
        halo_ref = refs[pos]; pos += 1
    w_refs = refs[pos:pos + len(pair_act)]; pos += len(pair_act)
    row_refs = refs[pos:pos + n_rows]; pos += n_rows
    tile_refs = refs[pos:pos + n_tiles]; pos += n_tiles
    out_refs = refs[pos:pos + n_out]
    accs = [jnp.dot(act_refs[ai][...], w[...], preferred_element_type=F32)
            for ai, w in zip(pair_act, w_refs)]
    haccs, keep = None, None
    if halo:
        haccs = [jnp.dot(halo_ref[...], w[...], preferred_element_type=F32) for w in w_refs]
        keep = jnp.where(pl.program_id(0) % tiles_per_seq != 0, 1.0, 0.0).astype(F32)
    epilogue(accs, haccs, keep, [r[...] for r in row_refs], tile_refs, out_refs)


def fused_mm(name, acts, pairs, epilogue, *, rows=(), tiles=(), out_dtypes, tm, tn, halo_seq=None):
    m = acts[0].shape[0]
    n = pairs[0][1].shape[1]
    tm = min(tm, m)
    tn = min(tn, n)
    assert m % tm == 0 and n % tn == 0, (name, m, tm, n, tn)
    in_specs, args = [], []
    for a in acts:
        in_specs.append(pl.BlockSpec((tm, a.shape[1]), lambda i, j: (i, 0)))
        args.append(a)
    halo = halo_seq is not None
    if halo:
        assert halo_seq % tm == 0 and tm % BF16_SUBLANES == 0
        step = tm // BF16_SUBLANES
        in_specs.append(pl.BlockSpec((BF16_SUBLANES, acts[0].shape[1]),
                                     lambda i, j: (jnp.maximum(i * step - 1, 0), 0)))
        args.append(acts[0])
    for _, w in pairs:
        in_specs.append(pl.BlockSpec((w.shape[0], tn), lambda i, j: (0, j)))
        args.append(w)
    for r in rows:
        in_specs.append(pl.BlockSpec((r.shape[0], tn), lambda i, j: (0, j)))
        args.append(r)
    for t, col0 in tiles:
        assert col0 % tn == 0
        in_specs.append(pl.BlockSpec((tm, tn), lambda i, j, off=col0 // tn: (i, j + off)))
        args.append(t)
    kern = functools.partial(
        _fused_mm_kernel, n_act=len(acts), pair_act=tuple(p[0] for p in pairs), n_rows=len(rows),
        n_tiles=len(tiles), n_out=len(out_dtypes), halo=halo,
        tiles_per_seq=(halo_seq // tm if halo else 1), epilogue=epilogue)
    return pl.pallas_call(
        kern,
        grid=(m // tm, n // tn),
        in_specs=in_specs,
        out_specs=[pl.BlockSpec((tm, tn), lambda i, j: (i, j)) for _ in out_dtypes],
        out_shape=[jax.ShapeDtypeStruct((m, n), dt) for dt in out_dtypes],
        compiler_params=_params(2),
        name=name,
    )(*args)


def _store_all(out_refs, y):
    for o in out_refs:
        o[...] = y.astype(o.dtype)


def _epi_plain(accs, haccs, keep, rows, tile_refs, out_refs):
    _store_all(out_refs, accs[0])


def _epi_headnorm(accs, haccs, keep, rows, tile_refs, out_refs):
    z, g = accs[0], rows[0]
    for c in range(z.shape[1] // HEAD_DIM):
        sl = slice(c * HEAD_DIM, (c + 1) * HEAD_DIM)
        blk = z[:, sl]
        y = blk * lax.rsqrt(jnp.mean(blk * blk, axis=-1, keepdims=True) + EPS) * g[:, sl]
        for o in out_refs:
            o[:, sl] = y.astype(o.dtype)


def _log_sigmoid(x):
    return -(jnp.maximum(-x, 0.0) + jnp.log1p(jnp.exp(-jnp.abs(x))))


def _epi_logf(accs, haccs, keep, rows, tile_refs, out_refs):
    _store_all(out_refs, _log_sigmoid(accs[0] + rows[0]))


def _epi_sigmoid(accs, haccs, keep, rows, tile_refs, out_refs):
    _store_all(out_refs, jax.nn.sigmoid(accs[0]))


def _causal_conv3(u, prev2, w):
    row = lax.broadcasted_iota(jnp.int32, u.shape, 0)
    u1 = jnp.where(row == 0, prev2[1:2], pltpu.roll(u, 1, 0))
    u2 = jnp.where(row == 0, prev2[0:1], jnp.where(row == 1, prev2[1:2], pltpu.roll(u, 2, 0)))
    y = u2 * w[0:1]
    y = y + u1 * w[1:2]
    return y + u * w[2:3]


def _epi_conv_branch(accs, haccs, keep, rows, tile_refs, out_refs):
    u = accs[1] * accs[2]
    hu = (haccs[1] * haccs[2])[BF16_SUBLANES - 2:BF16_SUBLANES] * keep
    _store_all(out_refs, accs[0] * _causal_conv3(u, hu, rows[0]))


def _epi_ffn_up(accs, haccs, keep, rows, tile_refs, out_refs):
    hg = haccs[0][BF16_SUBLANES - 2:BF16_SUBLANES] * keep
    gc = _causal_conv3(accs[0], hg, rows[0])
    _store_all(out_refs, gc * jax.nn.sigmoid(gc) * accs[1])


def _epi_merge(accs, haccs, keep, rows, tile_refs, out_refs):
    g_a, g_b, g_m = (t[...].astype(F32) for t in tile_refs)
    _store_all(out_refs, g_a * accs[0] + g_b * accs[1] + g_m * accs[2])


def _epi_residual(accs, haccs, keep, rows, tile_refs, out_refs):
    _store_all(out_refs, tile_refs[0][...] + accs[0])


def _cumsum_kernel(x_ref, o_ref, carry_ref, *, rows):
    @pl.when(pl.program_id(1) == 0)
    def _():
        carry_ref[...] = jnp.zeros_like(carry_ref)

    x = x_ref[...]
    row = lax.broadcasted_iota(jnp.int32, x.shape, 0)
    s = 1
    while s < rows:
        x = x + jnp.where(row >= s, pltpu.roll(x, s, 0), 0.0)
        s *= 2
    x = x + carry_ref[0:1, :]
    o_ref[...] = x
    carry_ref[...] = jnp.broadcast_to(x[rows - 1:rows, :], carry_ref.shape)


def seq_cumsum(x, n_seq, *, rows=512):
    m, w = x.shape
    per = m // n_seq // rows
    return pl.pallas_call(
        functools.partial(_cumsum_kernel, rows=rows),
        grid=(n_seq, per),
        in_specs=[pl.BlockSpec((rows, w), lambda b, i: (b * per + i, 0))],
        out_specs=pl.BlockSpec((rows, w), lambda b, i: (b * per + i, 0)),
        out_shape=jax.ShapeDtypeStruct((m, w), F32),
        scratch_shapes=[pltpu.VMEM((8, w), F32)],
        compiler_params=_params(2),
        name="logf_cumsum",
    )(x)


def _page_suffix_kernel(x_ref, o_ref, *, heads):
    x = x_ref[...]
    width = x.shape[1]
    lane = lax.broadcasted_iota(jnp.int32, x.shape, 1)
    inc = x
    s = heads
    while s < width:
        inc = inc + jnp.where(lane < width - s, pltpu.roll(inc, width - s, 1), 0.0)
        s *= 2
    o_ref[...] = inc - x


def page_suffix(logf_flat, *, tp=256):
    n, w = logf_flat.shape
    tp = min(tp, n)
    assert n % tp == 0
    return pl.pallas_call(
        functools.partial(_page_suffix_kernel, heads=FOX_HEADS),
        grid=(n // tp,),
        in_specs=[pl.BlockSpec((tp, w), lambda i: (i, 0))],
        out_specs=pl.BlockSpec((tp, w), lambda i: (i, 0)),
        out_shape=jax.ShapeDtypeStruct((n, w), F32),
        compiler_params=_params(1),
        name="page_logf_suffix",
    )(logf_flat)


def _token_conv(u, st_ref, w_ref, hist_ref):
    s0, s1 = st_ref[0], st_ref[1]
    w = w_ref[...]
    y = s0 * w[0:1]
    y = y + s1 * w[1:2]
    y = y + u * w[2:3]
    hist_ref[0] = s1
    hist_ref[1] = u
    return y


def _sample_conv_kernel(cb_ref, cc_ref, cx_ref, st_ref, w_ref, b_ref, hist_ref):
    y = _token_conv(cc_ref[...] * cx_ref[...], st_ref, w_ref, hist_ref)
    b_ref[...] = (cb_ref[...] * y).astype(b_ref.dtype)


def _sample_ffn_kernel(g_ref, v_ref, st_ref, w_ref, a_ref, hist_ref):
    y = _token_conv(g_ref[...], st_ref, w_ref, hist_ref)
    a_ref[...] = (y * jax.nn.sigmoid(y) * v_ref[...]).astype(a_ref.dtype)


def token_conv_call(name, body, vecs, state, w, *, tn):
    r, c = vecs[0].shape
    assert c % tn == 0 and tn % LANES == 0
    vec = pl.BlockSpec((r, tn), lambda j: (0, j))
    st = pl.BlockSpec((2, r, tn), lambda j: (0, 0, j))
    return pl.pallas_call(
        body,
        grid=(c // tn,),
        in_specs=[vec] * len(vecs) + [st, pl.BlockSpec((3, tn), lambda j: (0, j))],
        out_specs=[vec, st],
        out_shape=[jax.ShapeDtypeStruct((r, c), BF16), jax.ShapeDtypeStruct((2, r, c), F32)],
        compiler_params=_params(1),
        name=name,
    )(*vecs, state, w)


_NT = (((1,), (1,)), ((), ()))


def _fox_prompt_kernel(q_ref, k_ref, v_ref, ck_ref, o_ref, *, tq):
    i = pl.program_id(2)
    q = q_ref[...]

    def step(j, carry, masked):
        m, l, acc = carry
        start = pl.multiple_of(j * tq, tq)
        k = k_ref[pl.ds(start, tq), :]
        v = v_ref[pl.ds(start, tq), :]
        s = lax.dot_general(q, k, _NT, preferred_element_type=F32) - ck_ref[0, j]
        if masked:
            row = lax.broadcasted_iota(jnp.int32, s.shape, 0)
            col = lax.broadcasted_iota(jnp.int32, s.shape, 1)
            s = jnp.where(col <= row, s, -jnp.inf)
        m_new = jnp.maximum(m, jnp.max(s, axis=-1, keepdims=True))
        p = jnp.exp(s - m_new)
        alpha = jnp.exp(m - m_new)
        l = alpha * l + jnp.sum(p, axis=-1, keepdims=True)
        acc = alpha * acc + jnp.dot(p.astype(BF16), v, preferred_element_type=F32)
        return m_new, l, acc

    init = (jnp.full((tq, 1), -jnp.inf, F32), jnp.zeros((tq, 1), F32), jnp.zeros((tq, HEAD_DIM), F32))
    carry = lax.fori_loop(0, i, lambda j, c: step(j, c, False), init)
    m, l, acc = step(i, carry, True)
    o_ref[...] = (acc / l).astype(o_ref.dtype)


def fox_prompt_attention(q, k, v, c_rows, *, batch, seq, tq=512):
    nq = seq // tq
    return pl.pallas_call(
        functools.partial(_fox_prompt_kernel, tq=tq),
        grid=(batch, FOX_HEADS, nq),
        in_specs=[pl.BlockSpec((tq, HEAD_DIM), lambda b, h, i: (b * nq + i, h)),
                  pl.BlockSpec((seq, HEAD_DIM), lambda b, h, i: (b, h)),
                  pl.BlockSpec((seq, HEAD_DIM), lambda b, h, i: (b, h)),
                  pl.BlockSpec((1, nq, 1, tq), lambda b, h, i: (b * FOX_HEADS + h, 0, 0, 0))],
        out_specs=pl.BlockSpec((tq, HEAD_DIM), lambda b, h, i: (b * nq + i, h)),
        out_shape=jax.ShapeDtypeStruct(q.shape, BF16),
        compiler_params=_params(3),
        name="fox_prompt_attention",
    )(q, k, v, c_rows)


def _mem_attention_kernel(q_ref, k_ref, v_ref, o_ref):
    for h in range(MEM_HEADS):
        sl = slice(h * HEAD_DIM, (h + 1) * HEAD_DIM)
        q = q_ref[:, sl]
        k = k_ref[:, sl].astype(BF16)
        v = v_ref[:, sl].astype(BF16)
        s = lax.dot_general(q, k, _NT, preferred_element_type=F32)
        p = jnp.exp(s - jnp.max(s, axis=-1, keepdims=True))
        l = jnp.sum(p, axis=-1, keepdims=True)
        o = jnp.dot(p.astype(BF16), v, preferred_element_type=F32) / l
        o_ref[:, sl] = o.astype(o_ref.dtype)


def mem_attention(q, k, v, *, tq):
    rows = q.shape[0]
    groups = k.shape[0] // MEM_TOKENS
    per = rows // groups // tq
    width = q.shape[1]
    return pl.pallas_call(
        _mem_attention_kernel,
        grid=(groups, per),
        in_specs=[pl.BlockSpec((tq, width), lambda b, i: (b * per + i, 0)),
                  pl.BlockSpec((MEM_TOKENS, width), lambda b, i: (b, 0)),
                  pl.BlockSpec((MEM_TOKENS, width), lambda b, i: (b, 0))],
        out_specs=pl.BlockSpec((tq, width), lambda b, i: (b * per + i, 0)),
        out_shape=jax.ShapeDtypeStruct(q.shape, BF16),
        compiler_params=_params(2),
        name="mem_attention",
    )(q, k, v)


def _mem_attention_sample_kernel(q_ref, k_ref, v_ref, o_ref):
    for h in range(MEM_HEADS):
        sl = slice(h * HEAD_DIM, (h + 1) * HEAD_DIM)
        q = q_ref[0, :, sl]
        k = k_ref[:, sl].astype(BF16)
        v = v_ref[:, sl].astype(BF16)
        s = lax.dot_general(q, k, _NT, preferred_element_type=F32)
        p = jnp.exp(s - jnp.max(s, axis=-1, keepdims=True))
        l = jnp.sum(p, axis=-1, keepdims=True)
        o = jnp.dot(p.astype(BF16), v, preferred_element_type=F32) / l
        o_ref[0, :, sl] = o.astype(o_ref.dtype)


def mem_attention_sample(q, k, v):
    rows, _, width = q.shape
    return pl.pallas_call(
        _mem_attention_sample_kernel,
        grid=(rows,),
        in_specs=[pl.BlockSpec((1, 1, width), lambda b: (b, 0, 0)),
                  pl.BlockSpec((MEM_TOKENS, width), lambda b: (b, 0)),
                  pl.BlockSpec((MEM_TOKENS, width), lambda b: (b, 0))],
        out_specs=pl.BlockSpec((1, 1, width), lambda b: (b, 0, 0)),
        out_shape=jax.ShapeDtypeStruct(q.shape, BF16),
        compiler_params=_params(1),
        name="mem_attention_sample",
    )(q, k, v)


def _fox_decode_kernel(pt_ref, q_ref, kn_ref, vn_ref, fn_ref, k_ref, v_ref, e_ref, f_ref, o_ref,
                       m_ref, l_ref, run_ref, acc_ref, *, n_pages):
    j = pl.program_id(1)
    keys = PAGE * FOX_HEADS

    @pl.when(j == 0)
    def _():
        m_ref[...] = jnp.full_like(m_ref, -jnp.inf)
        l_ref[...] = jnp.zeros_like(l_ref)
        run_ref[...] = jnp.zeros_like(run_ref)
        acc_ref[...] = jnp.zeros_like(acc_ref)

    q = q_ref[0]
    k2 = k_ref[0, 0].reshape(keys, HEAD_DIM).astype(BF16)
    v2 = v_ref[0, 0].reshape(keys, HEAD_DIM).astype(BF16)
    x = lax.dot_general(q, k2, _NT, preferred_element_type=F32)
    head = lax.broadcasted_iota(jnp.int32, x.shape, 0)
    lane = lax.broadcasted_iota(jnp.int32, x.shape, 1)
    own = (lane % FOX_HEADS) == head
    e_row = e_ref[0]
    f_row = f_ref[0]
    run = run_ref[...]
    s = jnp.where(own, x + e_row + run, -jnp.inf)
    m_old = m_ref[...]
    m_new = jnp.maximum(m_old, jnp.max(s, axis=-1, keepdims=True))
    p = jnp.exp(s - m_new)
    alpha = jnp.exp(m_old - m_new)
    l_ref[...] = alpha * l_ref[...] + jnp.sum(p, axis=-1, keepdims=True)
    acc_ref[...] = alpha * acc_ref[...] + jnp.dot(p.astype(BF16), v2, preferred_element_type=F32)
    m_ref[...] = m_new
    first = own & (lane < FOX_HEADS)
    run_ref[...] = run + jnp.sum(jnp.where(first, e_row + f_row, 0.0), axis=-1, keepdims=True)

    @pl.when(j == n_pages - 1)
    def _():
        kn = kn_ref[0].astype(BF16).astype(F32)
        vn = vn_ref[0].astype(BF16).astype(F32)
        s_new = jnp.sum(q.astype(F32) * kn, axis=-1, keepdims=True) - fn_ref[0]
        m_p = m_ref[...]
        m_f = jnp.maximum(m_p, s_new)
        a_f = jnp.exp(m_p - m_f)
        p_new = jnp.exp(s_new - m_f)
        l_f = a_f * l_ref[...] + p_new
        p_new = p_new.astype(BF16).astype(F32)
        o_ref[0] = ((a_f * acc_ref[...] + p_new * vn) / l_f).astype(o_ref.dtype)


def fox_decode_attention(page_table, q, k_new, v_new, logf_new, cache_k, cache_v, e_flat, f_flat):
    rows, n_pages = page_table.shape
    keys = PAGE * FOX_HEADS
    tok = pl.BlockSpec((1, FOX_HEADS, HEAD_DIM), lambda b, j, pt: (b, 0, 0))
    page = pl.BlockSpec((1, 1, PAGE, FOX_HEADS, HEAD_DIM),
                        lambda b, j, pt: (0, pt[b, n_pages - 1 - j], 0, 0, 0))
    flat = pl.BlockSpec((1, 1, keys), lambda b, j, pt: (pt[b, n_pages - 1 - j], 0, 0))
    grid_spec = pltpu.PrefetchScalarGridSpec(
        num_scalar_prefetch=1,
        grid=(rows, n_pages),
        in_specs=[tok, tok, tok, pl.BlockSpec((1, FOX_HEADS, 1), lambda b, j, pt: (b, 0, 0)),
                  page, page, flat, flat],
        out_specs=tok,
        scratch_shapes=[pltpu.VMEM((FOX_HEADS, 1), F32), pltpu.VMEM((FOX_HEADS, 1), F32),
                        pltpu.VMEM((FOX_HEADS, 1), F32), pltpu.VMEM((FOX_HEADS, HEAD_DIM), F32)],
    )
    return pl.pallas_call(
        functools.partial(_fox_decode_kernel, n_pages=n_pages),
        grid_spec=grid_spec,
        out_shape=jax.ShapeDtypeStruct((rows, FOX_HEADS, HEAD_DIM), BF16),
        compiler_params=_params(2),
        name="fox_decode_attention",
    )(page_table, q, k_new, v_new, logf_new, cache_k, cache_v, e_flat, f_flat)


def _tile_gain(g, width, scale=1.0):
    return jnp.tile(g.astype(F32) * scale, width // HEAD_DIM).reshape(1, width)


def kernel(x_prompt, mem_prompt, x_sample, cache_k, cache_v, cache_logf, cache_mem_k, cache_mem_v,
           state_conv, state_ffn_conv, page_table, g_mix, w_in, b_f, q_norm_g, k_norm_g, mq_norm_g,
           mk_norm_g, g_mem, w_mem_kv, conv_w, w_o_fox, w_o_conv, w_o_mem, w_o, g_ffn, w_up,
           ffn_conv_w, w_down):
    depth = w_in.shape[0]
    assert depth == 1
    batch, seq, d_model = x_prompt.shape
    dec = x_sample.shape[0]
    assert x_sample.shape[1] == 1
    fox_w = FOX_HEADS * HEAD_DIM
    conv_c = state_conv.shape[-1]
    mem_w = MEM_HEADS * HEAD_DIM
    d_ff = w_down.shape[1]
    tokens = batch * seq
    l = 0

    o = 0
    def seg(width):
        nonlocal o
        w = w_in[l][:, o:o + width].astype(BF16)
        o += width
        return w
    w_q, w_k, w_v = seg(fox_w), seg(fox_w), seg(fox_w)
    w_f = jnp.pad(seg(FOX_HEADS), ((0, 0), (0, LANES - FOX_HEADS)))
    w_cb, w_cc, w_cx = seg(conv_c), seg(conv_c), seg(conv_c)
    w_mq = seg(mem_w)
    w_g = seg(3 * d_model)
    w_mk = w_mem_kv[l][:, :mem_w].astype(BF16)
    w_mv = w_mem_kv[l][:, mem_w:].astype(BF16)
    w_of, w_oc, w_om = w_o_fox[l].astype(BF16), w_o_conv[l].astype(BF16), w_o_mem[l].astype(BF16)
    w_oo = w_o[l].astype(BF16)
    w_ug, w_uv = w_up[l][:, :d_ff].astype(BF16), w_up[l][:, d_ff:].astype(BF16)
    w_dn = w_down[l].astype(BF16)
    b_f_row = jnp.pad(b_f[l].astype(F32), (0, LANES - FOX_HEADS)).reshape(1, LANES)
    g_q = _tile_gain(q_norm_g[l], fox_w, ATTN_SCALE)
    g_k = _tile_gain(k_norm_g[l], fox_w)
    g_mq = _tile_gain(mq_norm_g[l], mem_w, ATTN_SCALE)
    g_mk = _tile_gain(mk_norm_g[l], mem_w)
    cw, fcw = conv_w[l], ffn_conv_w[l]

    def proj(name, h, w, epi, rows=(), out_dtypes=(F32,), tm=1024, tn=512):
        return fused_mm(name, [h], [(0, w)], epi, rows=rows, out_dtypes=out_dtypes, tm=tm, tn=tn)

    xp = x_prompt.reshape(tokens, d_model)
    h = rmsnorm_bf16(xp, g_mix[l])
    (q_p,) = proj("p_q", h, w_q, _epi_headnorm, [g_q], (BF16,))
    k_p, k_p16 = proj("p_k", h, w_k, _epi_headnorm, [g_k], (F32, BF16))
    v_p, v_p16 = proj("p_v", h, w_v, _epi_plain, (), (F32, BF16))
    (logf_p,) = proj("p_logf", h, w_f, _epi_logf, [b_f_row], (F32,))
    (b_p,) = fused_mm("p_conv", [h], [(0, w_cb), (0, w_cc), (0, w_cx)], _epi_conv_branch,
                      rows=[cw], out_dtypes=(BF16,), tm=1024, tn=256, halo_seq=seq)
    (mq_p,) = proj("p_mq", h, w_mq, _epi_headnorm, [g_mq], (BF16,))
    (gates_p,) = proj("p_gates", h, w_g, _epi_sigmoid, (), (BF16,))

    hm = rmsnorm_bf16(mem_prompt.reshape(batch * MEM_TOKENS, d_model), g_mem[l])
    (mk_p,) = proj("p_mk", hm, w_mk, _epi_headnorm, [g_mk], (F32,), tm=512)
    (mv_p,) = proj("p_mv", hm, w_mv, _epi_plain, (), (F32,), tm=512)

    tq = 512
    c_p = seq_cumsum(logf_p, batch)[:, :FOX_HEADS]
    c_rows = c_p.reshape(batch, seq, FOX_HEADS).transpose(0, 2, 1).reshape(batch * FOX_HEADS, seq // tq, 1, tq)
    a_p = fox_prompt_attention(q_p, k_p16, v_p16, c_rows, batch=batch, seq=seq, tq=tq)
    m_p = mem_attention(mq_p, mk_p, mv_p, tq=1024)

    def merge_out(name, x, a, b, m, gates, tm):
        (merged,) = fused_mm(name + "_merge", [a, b, m], [(0, w_of), (1, w_oc), (2, w_om)], _epi_merge,
                             tiles=[(gates, 0), (gates, d_model), (gates, 2 * d_model)],
                             out_dtypes=(BF16,), tm=tm, tn=512)
        (y,) = fused_mm(name + "_out", [merged], [(0, w_oo)], _epi_residual, tiles=[(x, 0)],
                        out_dtypes=(F32,), tm=tm, tn=512)
        return y

    xp1 = merge_out("p", xp, a_p, b_p, m_p, gates_p, 1024)
    h2 = rmsnorm_bf16(xp1, g_ffn[l])
    (act_p,) = fused_mm("p_ffn_up", [h2], [(0, w_ug), (0, w_uv)], _epi_ffn_up, rows=[fcw],
                        out_dtypes=(BF16,), tm=1024, tn=256, halo_seq=seq)
    (y_p,) = fused_mm("p_ffn_down", [act_p], [(0, w_dn)], _epi_residual, tiles=[(xp1, 0)],
                      out_dtypes=(F32,), tm=512, tn=256)

    xs = x_sample.reshape(dec, d_model)
    tail = x_prompt[:, seq - 2:, :].reshape(batch * 2, d_model)
    n_small = dec + batch * 2
    pad_rows = (-n_small) % BF16_SUBLANES
    x_small = jnp.concatenate([xs, tail, jnp.zeros((pad_rows, d_model), F32)], axis=0)
    n_rows = x_small.shape[0]
    hs = rmsnorm_bf16(x_small, g_mix[l])
    sm = dict(tm=n_rows, tn=512)
    (q_s,) = proj("s_q", hs, w_q, _epi_headnorm, [g_q], (BF16,), **sm)
    (k_s,) = proj("s_k", hs, w_k, _epi_headnorm, [g_k], (F32,), **sm)
    (v_s,) = proj("s_v", hs, w_v, _epi_plain, (), (F32,), **sm)
    (logf_s,) = proj("s_logf", hs, w_f, _epi_logf, [b_f_row], (F32,), **sm)
    (cb_s,) = proj("s_cb", hs, w_cb, _epi_plain, (), (F32,), **sm)
    (cc_s,) = proj("s_cc", hs, w_cc, _epi_plain, (), (F32,), **sm)
    (cx_s,) = proj("s_cx", hs, w_cx, _epi_plain, (), (F32,), **sm)
    (mq_s,) = proj("s_mq", hs, w_mq, _epi_headnorm, [g_mq], (BF16,), **sm)
    (gates_s,) = proj("s_gates", hs, w_g, _epi_sigmoid, (), (BF16,), **sm)

    def small_state(state):
        return jnp.pad(state.transpose(1, 0, 2), ((0, 0), (0, n_rows - dec), (0, 0)))

    b_s, conv_hist = token_conv_call("s_conv", _sample_conv_kernel, [cb_s, cc_s, cx_s],
                                     small_state(state_conv[l]), cw, tn=conv_c // 2)
    conv_sample = conv_hist[:, :dec].transpose(1, 0, 2)
    conv_prompt = conv_hist[1, dec:n_small].reshape(batch, 2, conv_c)

    pool = cache_logf.shape[1]
    keys = PAGE * FOX_HEADS
    f_flat = cache_logf[l].reshape(pool, keys)
    e_flat = page_suffix(f_flat)
    k_new = k_s[:dec].reshape(dec, FOX_HEADS, HEAD_DIM)
    v_new = v_s[:dec].reshape(dec, FOX_HEADS, HEAD_DIM)
    logf_new = logf_s[:dec, :FOX_HEADS]
    a_s = fox_decode_attention(
        page_table, q_s[:dec].reshape(dec, FOX_HEADS, HEAD_DIM), k_new, v_new,
        logf_new.reshape(dec, FOX_HEADS, 1), cache_k[l:l + 1], cache_v[l:l + 1],
        e_flat.reshape(pool, 1, keys), f_flat.reshape(pool, 1, keys)).reshape(dec, fox_w)
    m_s = mem_attention_sample(mq_s[:dec].reshape(dec, 1, mem_w),
                               cache_mem_k[l].reshape(dec * MEM_TOKENS, mem_w),
                               cache_mem_v[l].reshape(dec * MEM_TOKENS, mem_w)).reshape(dec, mem_w)
    xs1 = merge_out("s", xs, a_s, b_s[:dec], m_s, gates_s[:dec], dec)

    tail1 = xp1.reshape(batch, seq, d_model)[:, seq - 2:, :].reshape(batch * 2, d_model)
    x_small2 = jnp.concatenate([xs1, tail1, jnp.zeros((pad_rows, d_model), F32)], axis=0)
    hs2 = rmsnorm_bf16(x_small2, g_ffn[l])
    (gate_s,) = proj("s_ffn_gate", hs2, w_ug, _epi_plain, (), (F32,), tm=n_rows, tn=256)
    (val_s,) = proj("s_ffn_val", hs2, w_uv, _epi_plain, (), (F32,), tm=n_rows, tn=256)
    act_s, ffn_hist = token_conv_call("s_ffn_act", _sample_ffn_kernel, [gate_s, val_s],
                                      small_state(state_ffn_conv[l]), fcw, tn=d_ff // 2)
    ffn_sample = ffn_hist[:, :dec].transpose(1, 0, 2)
    ffn_prompt = ffn_hist[1, dec:n_small].reshape(batch, 2, d_ff)
    (y_s,) = fused_mm("s_ffn_down", [act_s[:dec]], [(0, w_dn)], _epi_residual, tiles=[(xs1, 0)],
                      out_dtypes=(F32,), tm=dec, tn=256)

    pages = seq // PAGE
    return (y_p.reshape(batch, seq, d_model), y_s.reshape(dec, 1, d_model),
            k_p.reshape(1, batch, pages, PAGE, FOX_HEADS, HEAD_DIM),
            v_p.reshape(1, batch, pages, PAGE, FOX_HEADS, HEAD_DIM),
            logf_p[:, :FOX_HEADS].reshape(1, batch, pages, PAGE, FOX_HEADS),
            mk_p.reshape(1, batch, MEM_TOKENS, MEM_HEADS, HEAD_DIM),
            mv_p.reshape(1, batch, MEM_TOKENS, MEM_HEADS, HEAD_DIM),
            conv_prompt[None], ffn_prompt[None],
            k_new.reshape(1, dec, 1, FOX_HEADS, HEAD_DIM), v_new.reshape(1, dec, 1, FOX_HEADS, HEAD_DIM),
            logf_new.reshape(1, dec, 1, FOX_HEADS), conv_sample[None], ffn_sample[None])
```

```python
import functools

import jax
import jax.numpy as jnp
from jax import lax
from jax.experimental import pallas as pl
from jax.experimental.pallas import tpu as pltpu

F32 = jnp.float32
BF16 = jnp.bfloat16

EPS = 1e-6
HEAD_DIM = 128
LANES = 128
BF16_SUBLANES = 16
FOX_HEADS = 16
MEM_HEADS = 4
MEM_TOKENS = 256
PAGE = 128
ATTN_SCALE = HEAD_DIM ** -0.5
LOG2E = 1.4426950408889634
VMEM_LIMIT = 56 * 1024 * 1024


def _params(n_axes):
    return pltpu.CompilerParams(dimension_semantics=("arbitrary",) * n_axes,
                                vmem_limit_bytes=VMEM_LIMIT)


def _rmsnorm_kernel(x_ref, g_ref, o_ref):
    x = x_ref[...]
    y = x * lax.rsqrt(jnp.mean(x * x, axis=-1, keepdims=True) + EPS)
    o_ref[...] = (y * g_ref[...]).astype(o_ref.dtype)


def rmsnorm_bf16(x, g, *, tr=256):
    m, d = x.shape
    tr = min(tr, m)
    return pl.pallas_call(
        _rmsnorm_kernel,
        grid=(m // tr,),
        in_specs=[pl.BlockSpec((tr, d), lambda i: (i, 0)),
                  pl.BlockSpec((1, d), lambda i: (0, 0))],
        out_specs=pl.BlockSpec((tr, d), lambda i: (i, 0)),
        out_shape=jax.ShapeDtypeStruct((m, d), BF16),
        compiler_params=_params(1),
        name="rmsnorm",
    )(x, g.reshape(1, d))


def _fused_mm_kernel(*refs, n_act, pair_act, n_rows, n_tiles, n_out, halo, tiles_per_seq, epilogue):
    pos = 0
    act_refs = refs[pos:pos + n_act]; pos += n_act
    halo_ref = None
    if halo:
        halo_ref = refs[pos]; pos += 1
    w_refs = refs[pos:pos + len(pair_act)]; pos += len(pair_act)
    row_refs = refs[pos:pos + n_rows]; pos += n_rows
    tile_refs = refs[pos:pos + n_tiles]; pos += n_tiles
    out_refs = refs[pos:pos + n_out]
    accs = [jnp.dot(act_refs[ai][...], w[...], preferred_element_type=F32)
            for ai, w in zip(pair_act, w_refs)]
    haccs, keep = None, None
    if halo:
        haccs = [jnp.dot(halo_ref[...], w[...], preferred_element_type=F32) for w in w_refs]
        keep = jnp.where(pl.program_id(0) % tiles_per_seq != 0, 1.0, 0.0).astype(F32)
    epilogue(accs, haccs, keep, [r[...] for r in row_refs], tile_refs, out_refs)


def fused_mm(name, acts, pairs, epilogue, *, n, rows=(), tiles=(), out_dtypes, out_transposed=None,
             tm, tn, halo_seq=None):
    m = acts[0].shape[0]
    tm = min(tm, m)
    tn = min(tn, n)
    assert m % tm == 0 and n % tn == 0, (name, m, tm, n, tn)
    out_transposed = out_transposed or (False,) * len(out_dtypes)
    in_specs, args = [], []
    for a in acts:
        in_specs.append(pl.BlockSpec((tm, a.shape[1]), lambda i, j: (i, 0)))
        args.append(a)
    halo = halo_seq is not None
    if halo:
        assert halo_seq % tm == 0 and tm % BF16_SUBLANES == 0
        step = tm // BF16_SUBLANES
        in_specs.append(pl.BlockSpec((BF16_SUBLANES, acts[0].shape[1]),
                                     lambda i, j: (jnp.maximum(i * step - 1, 0), 0)))
        args.append(acts[0])
    for _, w, col0 in pairs:
        assert col0 % tn == 0
        in_specs.append(pl.BlockSpec((w.shape[0], tn), lambda i, j, off=col0 // tn: (0, j + off)))
        args.append(w)
    for r in rows:
        in_specs.append(pl.BlockSpec((r.shape[0], tn), lambda i, j: (0, j)))
        args.append(r)
    for t, col0 in tiles:
        assert col0 % tn == 0
        in_specs.append(pl.BlockSpec((tm, tn), lambda i, j, off=col0 // tn: (i, j + off)))
        args.append(t)
    kern = functools.partial(
        _fused_mm_kernel, n_act=len(acts), pair_act=tuple(p[0] for p in pairs), n_rows=len(rows),
        n_tiles=len(tiles), n_out=len(out_dtypes), halo=halo,
        tiles_per_seq=(halo_seq // tm if halo else 1), epilogue=epilogue)
    return pl.pallas_call(
        kern,
        grid=(m // tm, n // tn),
        in_specs=in_specs,
        out_specs=[pl.BlockSpec((tn, tm), lambda i, j: (j, i)) if tr else pl.BlockSpec((tm, tn), lambda i, j: (i, j))
                   for tr in out_transposed],
        out_shape=[jax.ShapeDtypeStruct((n, m) if tr else (m, n), dt)
                   for dt, tr in zip(out_dtypes, out_transposed)],
        compiler_params=_params(2),
        name=name,
    )(*args)


def _store_all(out_refs, y):
    for o in out_refs:
        o[...] = y.astype(o.dtype)


def _epi_plain(accs, haccs, keep, rows, tile_refs, out_refs):
    _store_all(out_refs, accs[0])


def _epi_plain_and_transposed(accs, haccs, keep, rows, tile_refs, out_refs):
    out_refs[0][...] = accs[0]
    out_refs[1][...] = accs[0].astype(out_refs[1].dtype).T


def _epi_headnorm(accs, haccs, keep, rows, tile_refs, out_refs):
    z, g = accs[0], rows[0]
    for c in range(z.shape[1] // HEAD_DIM):
        sl = slice(c * HEAD_DIM, (c + 1) * HEAD_DIM)
        blk = z[:, sl]
        y = blk * lax.rsqrt(jnp.mean(blk * blk, axis=-1, keepdims=True) + EPS) * g[:, sl]
        for o in out_refs:
            o[:, sl] = y.astype(o.dtype)


def _log_sigmoid(x):
    return -(jnp.maximum(-x, 0.0) + jnp.log1p(jnp.exp(-jnp.abs(x))))


def _epi_logf(accs, haccs, keep, rows, tile_refs, out_refs):
    _store_all(out_refs, _log_sigmoid(accs[0] + rows[0]))


def _epi_sigmoid(accs, haccs, keep, rows, tile_refs, out_refs):
    _store_all(out_refs, jax.nn.sigmoid(accs[0]))


def _causal_conv3(u, prev2, w):
    row = lax.broadcasted_iota(jnp.int32, u.shape, 0)
    u1 = jnp.where(row == 0, prev2[1:2], pltpu.roll(u, 1, 0))
    u2 = jnp.where(row == 0, prev2[0:1], jnp.where(row == 1, prev2[1:2], pltpu.roll(u, 2, 0)))
    y = u2 * w[0:1]
    y = y + u1 * w[1:2]
    return y + u * w[2:3]


def _epi_conv_branch(accs, haccs, keep, rows, tile_refs, out_refs):
    u = accs[1] * accs[2]
    hu = (haccs[1] * haccs[2])[BF16_SUBLANES - 2:BF16_SUBLANES] * keep
    _store_all(out_refs, accs[0] * _causal_conv3(u, hu, rows[0]))


def _epi_ffn_up(accs, haccs, keep, rows, tile_refs, out_refs):
    hg = haccs[0][BF16_SUBLANES - 2:BF16_SUBLANES] * keep
    gc = _causal_conv3(accs[0], hg, rows[0])
    _store_all(out_refs, gc * jax.nn.sigmoid(gc) * accs[1])


def _epi_merge(accs, haccs, keep, rows, tile_refs, out_refs):
    g_a, g_b, g_m = (t[...].astype(F32) for t in tile_refs)
    _store_all(out_refs, g_a * accs[0] + g_b * accs[1] + g_m * accs[2])


def _epi_residual(accs, haccs, keep, rows, tile_refs, out_refs):
    _store_all(out_refs, tile_refs[0][...] + accs[0])


def _cumsum_kernel(x_ref, o_ref, carry_ref, *, rows):
    @pl.when(pl.program_id(1) == 0)
    def _():
        carry_ref[...] = jnp.zeros_like(carry_ref)

    x = x_ref[...]
    row = lax.broadcasted_iota(jnp.int32, x.shape, 0)
    s = 1
    while s < rows:
        x = x + jnp.where(row >= s, pltpu.roll(x, s, 0), 0.0)
        s *= 2
    x = x + carry_ref[0:1, :]
    o_ref[...] = x
    carry_ref[...] = jnp.broadcast_to(x[rows - 1:rows, :], carry_ref.shape)


def seq_cumsum(x, n_seq, *, rows=512):
    m, w = x.shape
    per = m // n_seq // rows
    return pl.pallas_call(
        functools.partial(_cumsum_kernel, rows=rows),
        grid=(n_seq, per),
        in_specs=[pl.BlockSpec((rows, w), lambda b, i: (b * per + i, 0))],
        out_specs=pl.BlockSpec((rows, w), lambda b, i: (b * per + i, 0)),
        out_shape=jax.ShapeDtypeStruct((m, w), F32),
        scratch_shapes=[pltpu.VMEM((8, w), F32)],
        compiler_params=_params(2),
        name="logf_cumsum",
    )(x)


def _token_conv(u, st_ref, w_ref, hist_ref):
    s0, s1 = st_ref[0], st_ref[1]
    w = w_ref[...]
    y = s0 * w[0:1]
    y = y + s1 * w[1:2]
    y = y + u * w[2:3]
    hist_ref[0] = s1
    hist_ref[1] = u
    return y


def _sample_conv_kernel(cb_ref, cc_ref, cx_ref, st_ref, w_ref, b_ref, hist_ref):
    y = _token_conv(cc_ref[...] * cx_ref[...], st_ref, w_ref, hist_ref)
    b_ref[...] = (cb_ref[...] * y).astype(b_ref.dtype)


def _sample_ffn_kernel(g_ref, v_ref, st_ref, w_ref, a_ref, hist_ref):
    y = _token_conv(g_ref[...], st_ref, w_ref, hist_ref)
    a_ref[...] = (y * jax.nn.sigmoid(y) * v_ref[...]).astype(a_ref.dtype)


def token_conv_call(name, body, vecs, state, w, *, tn):
    r, c = vecs[0].shape
    assert c % tn == 0 and tn % LANES == 0
    vec = pl.BlockSpec((r, tn), lambda j: (0, j))
    st = pl.BlockSpec((2, r, tn), lambda j: (0, 0, j))
    return pl.pallas_call(
        body,
        grid=(c // tn,),
        in_specs=[vec] * len(vecs) + [st, pl.BlockSpec((3, tn), lambda j: (0, j))],
        out_specs=[vec, st],
        out_shape=[jax.ShapeDtypeStruct((r, c), BF16), jax.ShapeDtypeStruct((2, r, c), F32)],
        compiler_params=_params(1),
        name=name,
    )(*vecs, state, w)


_NT = (((1,), (1,)), ((), ()))


def _fox_prompt_kernel(q_ref, k_ref, vt_ref, cb_ref, o_ref, *, tq, qc):
    i = pl.program_id(2)
    n_chunks = tq // qc

    def step(j, carry, masked):
        start = pl.multiple_of(j * tq, tq)
        k = k_ref[pl.ds(start, tq), :]
        vt = vt_ref[:, pl.ds(start, tq)]
        cb = cb_ref[pl.ds(start, tq), :]
        cb = jnp.concatenate([cb] * (qc // LANES), axis=1)
        scores = []
        for c in range(n_chunks):
            q = q_ref[c * qc:(c + 1) * qc, :]
            s = lax.dot_general(k, q, _NT, preferred_element_type=F32) - cb
            if masked:
                key = lax.broadcasted_iota(jnp.int32, s.shape, 0)
                qry = lax.broadcasted_iota(jnp.int32, s.shape, 1) + c * qc
                s = jnp.where(key <= qry, s, -jnp.inf)
            scores.append(s)
        out = []
        for s, (m, l, acc) in zip(scores, carry):
            m_new = jnp.maximum(m, jnp.max(s, axis=0, keepdims=True))
            p = jnp.exp2(s - m_new)
            alpha = jnp.exp2(m - m_new)
            l = alpha * l + jnp.sum(p, axis=0, keepdims=True)
            acc = alpha * acc + jnp.dot(vt, p.astype(BF16), preferred_element_type=F32)
            out.append((m_new, l, acc))
        return tuple(out)

    init = tuple((jnp.full((1, qc), -jnp.inf, F32), jnp.zeros((1, qc), F32), jnp.zeros((HEAD_DIM, qc), F32))
                 for _ in range(n_chunks))
    carry = lax.fori_loop(0, i, lambda j, c: step(j, c, False), init)
    carry = step(i, carry, True)
    for c, (m, l, acc) in enumerate(carry):
        o_ref[c * qc:(c + 1) * qc, :] = (acc / l).T.astype(o_ref.dtype)


def fox_prompt_attention(q, k, vt, cb, *, batch, seq, tq=512, qc=512):
    nq = seq // tq
    return pl.pallas_call(
        functools.partial(_fox_prompt_kernel, tq=tq, qc=qc),
        grid=(batch, FOX_HEADS, nq),
        in_specs=[pl.BlockSpec((tq, HEAD_DIM), lambda b, h, i: (b * nq + i, h)),
                  pl.BlockSpec((seq, HEAD_DIM), lambda b, h, i: (b, h)),
                  pl.BlockSpec((HEAD_DIM, seq), lambda b, h, i: (h, b)),
                  pl.BlockSpec((None, seq, LANES), lambda b, h, i: (b * FOX_HEADS + h, 0, 0))],
        out_specs=pl.BlockSpec((tq, HEAD_DIM), lambda b, h, i: (b * nq + i, h)),
        out_shape=jax.ShapeDtypeStruct(q.shape, BF16),
        compiler_params=_params(3),
        name="fox_prompt_attention",
    )(q, k, vt, cb)


def _mem_attention_kernel(q_ref, k_ref, v_ref, o_ref):
    for h in range(MEM_HEADS):
        sl = slice(h * HEAD_DIM, (h + 1) * HEAD_DIM)
        q = q_ref[:, sl]
        k = k_ref[:, sl].astype(BF16)
        v = v_ref[:, sl].astype(BF16)
        s = lax.dot_general(q, k, _NT, preferred_element_type=F32)
        p = jnp.exp(s - jnp.max(s, axis=-1, keepdims=True))
        l = jnp.sum(p, axis=-1, keepdims=True)
        o = jnp.dot(p.astype(BF16), v, preferred_element_type=F32) / l
        o_ref[:, sl] = o.astype(o_ref.dtype)


def mem_attention(q, k, v, *, tq):
    rows = q.shape[0]
    groups = k.shape[0] // MEM_TOKENS
    per = rows // groups // tq
    width = q.shape[1]
    return pl.pallas_call(
        _mem_attention_kernel,
        grid=(groups, per),
        in_specs=[pl.BlockSpec((tq, width), lambda b, i: (b * per + i, 0)),
                  pl.BlockSpec((MEM_TOKENS, width), lambda b, i: (b, 0)),
                  pl.BlockSpec((MEM_TOKENS, width), lambda b, i: (b, 0))],
        out_specs=pl.BlockSpec((tq, width), lambda b, i: (b * per + i, 0)),
        out_shape=jax.ShapeDtypeStruct(q.shape, BF16),
        compiler_params=_params(2),
        name="mem_attention",
    )(q, k, v)


def _mem_attention_sample_kernel(q_ref, k_ref, v_ref, o_ref):
    for h in range(MEM_HEADS):
        sl = slice(h * HEAD_DIM, (h + 1) * HEAD_DIM)
        q = q_ref[0, :, sl]
        k = k_ref[:, sl].astype(BF16)
        v = v_ref[:, sl].astype(BF16)
        s = lax.dot_general(q, k, _NT, preferred_element_type=F32)
        p = jnp.exp(s - jnp.max(s, axis=-1, keepdims=True))
        l = jnp.sum(p, axis=-1, keepdims=True)
        o = jnp.dot(p.astype(BF16), v, preferred_element_type=F32) / l
        o_ref[0, :, sl] = o.astype(o_ref.dtype)


def mem_attention_sample(q, k, v):
    rows, _, width = q.shape
    return pl.pallas_call(
        _mem_attention_sample_kernel,
        grid=(rows,),
        in_specs=[pl.BlockSpec((1, 1, width), lambda b: (b, 0, 0)),
                  pl.BlockSpec((MEM_TOKENS, width), lambda b: (b, 0)),
                  pl.BlockSpec((MEM_TOKENS, width), lambda b: (b, 0))],
        out_specs=pl.BlockSpec((1, 1, width), lambda b: (b, 0, 0)),
        out_shape=jax.ShapeDtypeStruct(q.shape, BF16),
        compiler_params=_params(1),
        name="mem_attention_sample",
    )(q, k, v)


def _fox_decode_kernel(pt_ref, q_ref, kn_ref, vn_ref, fn_ref, u_ref, *refs, n_steps, pp):
    k_refs, v_refs, f_refs = refs[:pp], refs[pp:2 * pp], refs[2 * pp:3 * pp]
    o_ref, m_ref, l_ref, run_ref, acc_ref = refs[3 * pp:]
    j = pl.program_id(1)
    keys = PAGE * FOX_HEADS

    @pl.when(j == 0)
    def _():
        m_ref[...] = jnp.full_like(m_ref, -jnp.inf)
        l_ref[...] = jnp.zeros_like(l_ref)
        run_ref[...] = jnp.zeros_like(run_ref)
        acc_ref[...] = jnp.zeros_like(acc_ref)

    q = q_ref[0]
    head = lax.broadcasted_iota(jnp.int32, (FOX_HEADS, keys), 0)
    lane = lax.broadcasted_iota(jnp.int32, (FOX_HEADS, keys), 1)
    own = (lane % FOX_HEADS) == head

    splits, totals = [], []
    for f_ref in f_refs:
        ft = f_ref[0, 0].T
        hi = ft.astype(BF16)
        rest = ft - hi.astype(F32)
        mid = rest.astype(BF16)
        lo = (rest - mid.astype(F32)).astype(BF16)
        splits += [hi, mid, lo]
        totals.append(jnp.sum(ft, axis=-1, keepdims=True))
    e_all = jnp.dot(jnp.concatenate(splits, axis=0), u_ref[...], preferred_element_type=F32)

    run = run_ref[...]
    scores = []
    for i, k_ref in enumerate(k_refs):
        k2 = k_ref[0, 0].reshape(keys, HEAD_DIM).astype(BF16)
        x = lax.dot_general(q, k2, _NT, preferred_element_type=F32)
        r0 = 3 * FOX_HEADS * i
        e = (e_all[r0:r0 + FOX_HEADS] + e_all[r0 + FOX_HEADS:r0 + 2 * FOX_HEADS]
             + e_all[r0 + 2 * FOX_HEADS:r0 + 3 * FOX_HEADS])
        scores.append(jnp.where(own, x + e + run, -jnp.inf))
        run = run + totals[i]
    run_ref[...] = run

    m_old = m_ref[...]
    m_new = m_old
    for s in scores:
        m_new = jnp.maximum(m_new, jnp.max(s, axis=-1, keepdims=True))
    alpha = jnp.exp(m_old - m_new)
    l = alpha * l_ref[...]
    acc = alpha * acc_ref[...]
    for s, v_ref in zip(scores, v_refs):
        p = jnp.exp(s - m_new)
        l = l + jnp.sum(p, axis=-1, keepdims=True)
        v2 = v_ref[0, 0].reshape(keys, HEAD_DIM).astype(BF16)
        acc = acc + jnp.dot(p.astype(BF16), v2, preferred_element_type=F32)
    l_ref[...] = l
    acc_ref[...] = acc
    m_ref[...] = m_new

    @pl.when(j == n_steps - 1)
    def _():
        kn = kn_ref[0].astype(BF16).astype(F32)
        vn = vn_ref[0].astype(BF16).astype(F32)
        s_new = jnp.sum(q.astype(F32) * kn, axis=-1, keepdims=True) - fn_ref[0]
        m_p = m_ref[...]
        m_f = jnp.maximum(m_p, s_new)
        a_f = jnp.exp(m_p - m_f)
        p_new = jnp.exp(s_new - m_f)
        l_f = a_f * l_ref[...] + p_new
        p_new = p_new.astype(BF16).astype(F32)
        o_ref[0] = ((a_f * acc_ref[...] + p_new * vn) / l_f).astype(o_ref.dtype)


def fox_decode_attention(page_table, q, k_new, v_new, logf_new, cache_k, cache_v, cache_logf, *, pp=8):
    rows, n_pages = page_table.shape
    keys = PAGE * FOX_HEADS
    assert n_pages % pp == 0
    n_steps = n_pages // pp
    later = (lax.broadcasted_iota(jnp.int32, (PAGE, keys), 0)
             > lax.broadcasted_iota(jnp.int32, (PAGE, keys), 1) // FOX_HEADS).astype(BF16)
    tok = pl.BlockSpec((1, FOX_HEADS, HEAD_DIM), lambda b, j, pt: (b, 0, 0))

    def page_spec(i, tail):
        return pl.BlockSpec((1, 1, PAGE) + tail,
                            lambda b, j, pt: (0, pt[b, n_pages - 1 - (j * pp + i)]) + (0,) * (1 + len(tail)))

    grid_spec = pltpu.PrefetchScalarGridSpec(
        num_scalar_prefetch=1,
        grid=(rows, n_steps),
        in_specs=([tok, tok, tok, pl.BlockSpec((1, FOX_HEADS, 1), lambda b, j, pt: (b, 0, 0)),
                   pl.BlockSpec((PAGE, keys), lambda b, j, pt: (0, 0))]
                  + [page_spec(i, (FOX_HEADS, HEAD_DIM)) for i in range(pp)] * 2
                  + [page_spec(i, (FOX_HEADS,)) for i in range(pp)]),
        out_specs=tok,
        scratch_shapes=[pltpu.VMEM((FOX_HEADS, 1), F32), pltpu.VMEM((FOX_HEADS, 1), F32),
                        pltpu.VMEM((FOX_HEADS, 1), F32), pltpu.VMEM((FOX_HEADS, HEAD_DIM), F32)],
    )
    return pl.pallas_call(
        functools.partial(_fox_decode_kernel, n_steps=n_steps, pp=pp),
        grid_spec=grid_spec,
        out_shape=jax.ShapeDtypeStruct((rows, FOX_HEADS, HEAD_DIM), BF16),
        compiler_params=_params(2),
        name="fox_decode_attention",
    )(page_table, q, k_new, v_new, logf_new, later, *([cache_k] * pp), *([cache_v] * pp), *([cache_logf] * pp))


def _tile_gain(g, width, scale=1.0):
    return jnp.tile(g.astype(F32) * scale, width // HEAD_DIM).reshape(1, width)


def kernel(x_prompt, mem_prompt, x_sample, cache_k, cache_v, cache_logf, cache_mem_k, cache_mem_v,
           state_conv, state_ffn_conv, page_table, g_mix, w_in, b_f, q_norm_g, k_norm_g, mq_norm_g,
           mk_norm_g, g_mem, w_mem_kv, conv_w, w_o_fox, w_o_conv, w_o_mem, w_o, g_ffn, w_up,
           ffn_conv_w, w_down):
    depth = w_in.shape[0]
    assert depth == 1
    batch, seq, d_model = x_prompt.shape
    dec = x_sample.shape[0]
    assert x_sample.shape[1] == 1
    fox_w = FOX_HEADS * HEAD_DIM
    conv_c = state_conv.shape[-1]
    mem_w = MEM_HEADS * HEAD_DIM
    d_ff = w_down.shape[1]
    tokens = batch * seq
    l = 0

    f0 = 3 * fox_w
    g0 = w_in.shape[2] - 3 * d_model
    w_cat = jnp.concatenate(
        [w_in[l][:, g0:], w_in[l][:, :f0], w_in[l][:, f0 + FOX_HEADS:g0], w_in[l][:, f0:f0 + FOX_HEADS],
         jnp.zeros((d_model, LANES - FOX_HEADS), w_in.dtype)], axis=1).astype(BF16)
    o = 0
    def seg(width):
        nonlocal o
        w = (w_cat, o, width)
        o += width
        return w
    w_g = seg(3 * d_model)
    w_q, w_k, w_v = seg(fox_w), seg(fox_w), seg(fox_w)
    w_cb, w_cc, w_cx = seg(conv_c), seg(conv_c), seg(conv_c)
    w_mq = seg(mem_w)
    w_f = seg(LANES)
    w_mkv = w_mem_kv[l].astype(BF16)
    w_mk, w_mv = (w_mkv, 0, mem_w), (w_mkv, mem_w, mem_w)
    w_of, w_oc, w_om = ((w[l].astype(BF16), 0, d_model) for w in (w_o_fox, w_o_conv, w_o_mem))
    w_oo = (w_o[l].astype(BF16), 0, d_model)
    w_u = w_up[l].astype(BF16)
    w_ug, w_uv = (w_u, 0, d_ff), (w_u, d_ff, d_ff)
    w_dn = (w_down[l].astype(BF16), 0, d_model)
    b_f_row = jnp.pad(b_f[l].astype(F32), (0, LANES - FOX_HEADS)).reshape(1, LANES)
    g_q2 = _tile_gain(q_norm_g[l], fox_w, ATTN_SCALE * LOG2E)
    g_q = _tile_gain(q_norm_g[l], fox_w, ATTN_SCALE)
    g_k = _tile_gain(k_norm_g[l], fox_w)
    g_mq = _tile_gain(mq_norm_g[l], mem_w, ATTN_SCALE)
    g_mk = _tile_gain(mk_norm_g[l], mem_w)
    cw, fcw = conv_w[l], ffn_conv_w[l]

    def mm(name, acts, ws, epi, **kw):
        return fused_mm(name, acts, [(ai, w[0], w[1]) for ai, w in ws], epi, n=ws[0][1][2], **kw)

    def proj(name, h, w, epi, rows=(), out_dtypes=(F32,), tm=1024, tn=512, **kw):
        return mm(name, [h], [(0, w)], epi, rows=rows, out_dtypes=out_dtypes, tm=tm, tn=tn, **kw)

    xp = x_prompt.reshape(tokens, d_model)
    h = rmsnorm_bf16(xp, g_mix[l])
    (q_p,) = proj("p_q", h, w_q, _epi_headnorm, [g_q2], (BF16,), tn=1024)
    k_p, k_p16 = proj("p_k", h, w_k, _epi_headnorm, [g_k], (F32, BF16))
    v_p, vt_p16 = proj("p_v", h, w_v, _epi_plain_and_transposed, (), (F32, BF16), out_transposed=(False, True))
    (logf_p,) = proj("p_logf", h, w_f, _epi_logf, [b_f_row], (F32,))
    (b_p,) = mm("p_conv", [h], [(0, w_cb), (0, w_cc), (0, w_cx)], _epi_conv_branch,
                rows=[cw], out_dtypes=(BF16,), tm=1024, tn=256, halo_seq=seq)
    (mq_p,) = proj("p_mq", h, w_mq, _epi_headnorm, [g_mq], (BF16,))
    (gates_p,) = proj("p_gates", h, w_g, _epi_sigmoid, (), (BF16,), tn=1024)

    hm = rmsnorm_bf16(mem_prompt.reshape(batch * MEM_TOKENS, d_model), g_mem[l])
    (mk_p,) = proj("p_mk", hm, w_mk, _epi_headnorm, [g_mk], (F32,), tm=512)
    (mv_p,) = proj("p_mv", hm, w_mv, _epi_plain, (), (F32,), tm=512)

    c_p = seq_cumsum(logf_p, batch)[:, :FOX_HEADS] * LOG2E
    c_lanes = jnp.broadcast_to(c_p.reshape(batch, seq, FOX_HEADS).transpose(0, 2, 1)[..., None],
                               (batch, FOX_HEADS, seq, LANES)).reshape(batch * FOX_HEADS, seq, LANES)
    a_p = fox_prompt_attention(q_p, k_p16, vt_p16, c_lanes, batch=batch, seq=seq)
    m_p = mem_attention(mq_p, mk_p, mv_p, tq=1024)

    def merge_out(name, x, a, b, m, gates, tm):
        (merged,) = mm(name + "_merge", [a, b, m], [(0, w_of), (1, w_oc), (2, w_om)], _epi_merge,
                       tiles=[(gates, 0), (gates, d_model), (gates, 2 * d_model)],
                       out_dtypes=(BF16,), tm=tm, tn=512)
        (y,) = mm(name + "_out", [merged], [(0, w_oo)], _epi_residual, tiles=[(x, 0)],
                  out_dtypes=(F32,), tm=tm, tn=512)
        return y

    xp1 = merge_out("p", xp, a_p, b_p, m_p, gates_p, 1024)
    h2 = rmsnorm_bf16(xp1, g_ffn[l])
    (act_p,) = mm("p_ffn_up", [h2], [(0, w_ug), (0, w_uv)], _epi_ffn_up, rows=[fcw],
                  out_dtypes=(BF16,), tm=1024, tn=256, halo_seq=seq)
    (y_p,) = mm("p_ffn_down", [act_p], [(0, w_dn)], _epi_residual, tiles=[(xp1, 0)],
                out_dtypes=(F32,), tm=512, tn=256)

    xs = x_sample.reshape(dec, d_model)
    tail = x_prompt[:, seq - 2:, :].reshape(batch * 2, d_model)
    n_small = dec + batch * 2
    pad_rows = (-n_small) % BF16_SUBLANES
    x_small = jnp.concatenate([xs, tail, jnp.zeros((pad_rows, d_model), F32)], axis=0)
    n_rows = x_small.shape[0]
    hs = rmsnorm_bf16(x_small, g_mix[l])
    sm = dict(tm=n_rows, tn=512)
    (q_s,) = proj("s_q", hs, w_q, _epi_headnorm, [g_q], (BF16,), **sm)
    (k_s,) = proj("s_k", hs, w_k, _epi_headnorm, [g_k], (F32,), **sm)
    (v_s,) = proj("s_v", hs, w_v, _epi_plain, (), (F32,), **sm)
    (logf_s,) = proj("s_logf", hs, w_f, _epi_logf, [b_f_row], (F32,), **sm)
    (cb_s,) = proj("s_cb", hs, w_cb, _epi_plain, (), (F32,), **sm)
    (cc_s,) = proj("s_cc", hs, w_cc, _epi_plain, (), (F32,), **sm)
    (cx_s,) = proj("s_cx", hs, w_cx, _epi_plain, (), (F32,), **sm)
    (mq_s,) = proj("s_mq", hs, w_mq, _epi_headnorm, [g_mq], (BF16,), **sm)
    (gates_s,) = proj("s_gates", hs, w_g, _epi_sigmoid, (), (BF16,), **sm)

    def small_state(state):
        return jnp.pad(state.transpose(1, 0, 2), ((0, 0), (0, n_rows - dec), (0, 0)))

    b_s, conv_hist = token_conv_call("s_conv", _sample_conv_kernel, [cb_s, cc_s, cx_s],
                                     small_state(state_conv[l]), cw, tn=conv_c // 2)
    conv_sample = conv_hist[:, :dec].transpose(1, 0, 2)
    conv_prompt = conv_hist[1, dec:n_small].reshape(batch, 2, conv_c)

    k_new = k_s[:dec].reshape(dec, FOX_HEADS, HEAD_DIM)
    v_new = v_s[:dec].reshape(dec, FOX_HEADS, HEAD_DIM)
    logf_new = logf_s[:dec, :FOX_HEADS]
    a_s = fox_decode_attention(
        page_table, q_s[:dec].reshape(dec, FOX_HEADS, HEAD_DIM), k_new, v_new,
        logf_new.reshape(dec, FOX_HEADS, 1), cache_k[l:l + 1], cache_v[l:l + 1],
        cache_logf[l:l + 1]).reshape(dec, fox_w)
    m_s = mem_attention_sample(mq_s[:dec].reshape(dec, 1, mem_w),
                               cache_mem_k[l].reshape(dec * MEM_TOKENS, mem_w),
                               cache_mem_v[l].reshape(dec * MEM_TOKENS, mem_w)).reshape(dec, mem_w)
    xs1 = merge_out("s", xs, a_s, b_s[:dec], m_s, gates_s[:dec], dec)

    tail1 = xp1.reshape(batch, seq, d_model)[:, seq - 2:, :].reshape(batch * 2, d_model)
    x_small2 = jnp.concatenate([xs1, tail1, jnp.zeros((pad_rows, d_model), F32)], axis=0)
    hs2 = rmsnorm_bf16(x_small2, g_ffn[l])
    (gate_s,) = proj("s_ffn_gate", hs2, w_ug, _epi_plain, (), (F32,), tm=n_rows, tn=256)
    (val_s,) = proj("s_ffn_val", hs2, w_uv, _epi_plain, (), (F32,), tm=n_rows, tn=256)
    act_s, ffn_hist = token_conv_call("s_ffn_act", _sample_ffn_kernel, [gate_s, val_s],
                                      small_state(state_ffn_conv[l]), fcw, tn=d_ff // 2)
    ffn_sample = ffn_hist[:, :dec].transpose(1, 0, 2)
    ffn_prompt = ffn_hist[1, dec:n_small].reshape(batch, 2, d_ff)
    (y_s,) = mm("s_ffn_down", [act_s[:dec]], [(0, w_dn)], _epi_residual, tiles=[(xs1, 0)],
                out_dtypes=(F32,), tm=dec, tn=256)

    pages = seq // PAGE
    return (y_p.reshape(batch, seq, d_model), y_s.reshape(dec, 1, d_model),
            k_p.reshape(1, batch, pages, PAGE, FOX_HEADS, HEAD_DIM),
            v_p.reshape(1, batch, pages, PAGE, FOX_HEADS, HEAD_DIM),
            logf_p[:, :FOX_HEADS].reshape(1, batch, pages, PAGE, FOX_HEADS),
            mk_p.reshape(1, batch, MEM_TOKENS, MEM_HEADS, HEAD_DIM),
            mv_p.reshape(1, batch, MEM_TOKENS, MEM_HEADS, HEAD_DIM),
            conv_prompt[None], ffn_prompt[None],
            k_new.reshape(1, dec, 1, FOX_HEADS, HEAD_DIM), v_new.reshape(1, dec, 1, FOX_HEADS, HEAD_DIM),
            logf_new.reshape(1, dec, 1, FOX_HEADS), conv_sample[None], ffn_sample[None])
```

```python
import functools

import jax
import jax.numpy as jnp
from jax import lax
from jax.experimental import pallas as pl
from jax.experimental.pallas import tpu as pltpu

F32 = jnp.float32
BF16 = jnp.bfloat16

EPS = 1e-6
HEAD_DIM = 128
LANES = 128
BF16_SUBLANES = 16
FOX_HEADS = 16
MEM_HEADS = 4
MEM_TOKENS = 256
PAGE = 128
ATTN_SCALE = HEAD_DIM ** -0.5
LOG2E = 1.4426950408889634
VMEM_LIMIT = 56 * 1024 * 1024


def _params(n_axes):
    return pltpu.CompilerParams(dimension_semantics=("arbitrary",) * n_axes,
                                vmem_limit_bytes=VMEM_LIMIT)


def _rmsnorm_kernel(x_ref, g_ref, o_ref):
    x = x_ref[...]
    y = x * lax.rsqrt(jnp.mean(x * x, axis=-1, keepdims=True) + EPS)
    o_ref[...] = (y * g_ref[...]).astype(o_ref.dtype)


def rmsnorm_bf16(x, g, *, tr=256):
    m, d = x.shape
    tr = min(tr, m)
    return pl.pallas_call(
        _rmsnorm_kernel,
        grid=(m // tr,),
        in_specs=[pl.BlockSpec((tr, d), lambda i: (i, 0)),
                  pl.BlockSpec((1, d), lambda i: (0, 0))],
        out_specs=pl.BlockSpec((tr, d), lambda i: (i, 0)),
        out_shape=jax.ShapeDtypeStruct((m, d), BF16),
        compiler_params=_params(1),
        name="rmsnorm",
    )(x, g.reshape(1, d))


def _cast_transpose_kernel(x_ref, o_ref):
    o_ref[...] = x_ref[...].astype(o_ref.dtype).T


def cast_transpose_rows(wt, pieces, *, tn=512):
    k = wt.shape[1]
    starts, total = [], 0
    for row0, n_blocks, _ in pieces:
        assert row0 % BF16_SUBLANES == 0 and row0 + n_blocks * tn <= wt.shape[0]
        starts.append(total)
        total += n_blocks

    def pick(j, fn):
        out = 0
        for (row0, n_blocks, dest0), s in zip(pieces, starts):
            out = out + jnp.where((j >= s) & (j < s + n_blocks), fn(row0, dest0, j - s), 0)
        return out

    return pl.pallas_call(
        _cast_transpose_kernel,
        grid=(total,),
        in_specs=[pl.BlockSpec(
            (pl.Element(tn), pl.Element(k)),
            lambda j: (pl.multiple_of(pick(j, lambda row0, dest0, t: row0 + t * tn), BF16_SUBLANES), 0))],
        out_specs=pl.BlockSpec((k, tn), lambda j: (0, pick(j, lambda row0, dest0, t: dest0 + t))),
        out_shape=jax.ShapeDtypeStruct((k, total * tn), BF16),
        compiler_params=_params(1),
        name="cast_transpose_rows",
    )(wt)


def _fused_mm_kernel(*refs, n_act, pair_act, n_rows, n_tiles, n_out, halo, tiles_per_seq, epilogue):
    pos = 0
    act_refs = refs[pos:pos + n_act]; pos += n_act
    halo_ref = None
    if halo:
        halo_ref = refs[pos]; pos += 1
    w_refs = refs[pos:pos + len(pair_act)]; pos += len(pair_act)
    row_refs = refs[pos:pos + n_rows]; pos += n_rows
    tile_refs = refs[pos:pos + n_tiles]; pos += n_tiles
    out_refs = refs[pos:pos + n_out]
    accs = [jnp.dot(act_refs[ai][...], w[...], preferred_element_type=F32)
            for ai, w in zip(pair_act, w_refs)]
    haccs, keep = None, None
    if halo:
        haccs = [jnp.dot(halo_ref[...], w[...], preferred_element_type=F32) for w in w_refs]
        keep = jnp.where(pl.program_id(0) % tiles_per_seq != 0, 1.0, 0.0).astype(F32)
    epilogue(accs, haccs, keep, [r[...] for r in row_refs], tile_refs, out_refs)


def fused_mm(name, acts, pairs, epilogue, *, n, rows=(), tiles=(), out_dtypes, out_transposed=None,
             tm, tn, halo_seq=None):
    m = acts[0].shape[0]
    tm = min(tm, m)
    tn = min(tn, n)
    assert m % tm == 0 and n % tn == 0, (name, m, tm, n, tn)
    out_transposed = out_transposed or (False,) * len(out_dtypes)
    in_specs, args = [], []
    for a in acts:
        in_specs.append(pl.BlockSpec((tm, a.shape[1]), lambda i, j: (i, 0)))
        args.append(a)
    halo = halo_seq is not None
    if halo:
        assert halo_seq % tm == 0 and tm % BF16_SUBLANES == 0
        step = tm // BF16_SUBLANES
        in_specs.append(pl.BlockSpec((BF16_SUBLANES, acts[0].shape[1]),
                                     lambda i, j: (jnp.maximum(i * step - 1, 0), 0)))
        args.append(acts[0])
    for _, w, col0 in pairs:
        assert col0 % tn == 0
        in_specs.append(pl.BlockSpec((w.shape[0], tn), lambda i, j, off=col0 // tn: (0, j + off)))
        args.append(w)
    for r in rows:
        in_specs.append(pl.BlockSpec((r.shape[0], tn), lambda i, j: (0, j)))
        args.append(r)
    for t, col0 in tiles:
        assert col0 % tn == 0
        in_specs.append(pl.BlockSpec((tm, tn), lambda i, j, off=col0 // tn: (i, j + off)))
        args.append(t)
    kern = functools.partial(
        _fused_mm_kernel, n_act=len(acts), pair_act=tuple(p[0] for p in pairs), n_rows=len(rows),
        n_tiles=len(tiles), n_out=len(out_dtypes), halo=halo,
        tiles_per_seq=(halo_seq // tm if halo else 1), epilogue=epilogue)
    return pl.pallas_call(
        kern,
        grid=(m // tm, n // tn),
        in_specs=in_specs,
        out_specs=[pl.BlockSpec((tn, tm), lambda i, j: (j, i)) if tr else pl.BlockSpec((tm, tn), lambda i, j: (i, j))
                   for tr in out_transposed],
        out_shape=[jax.ShapeDtypeStruct((n, m) if tr else (m, n), dt)
                   for dt, tr in zip(out_dtypes, out_transposed)],
        compiler_params=_params(2),
        name=name,
    )(*args)


def _store_all(out_refs, y):
    for o in out_refs:
        o[...] = y.astype(o.dtype)


def _epi_plain(accs, haccs, keep, rows, tile_refs, out_refs):
    _store_all(out_refs, accs[0])


def _epi_plain_and_transposed(accs, haccs, keep, rows, tile_refs, out_refs):
    out_refs[0][...] = accs[0]
    out_refs[1][...] = accs[0].astype(out_refs[1].dtype).T


def _epi_headnorm(accs, haccs, keep, rows, tile_refs, out_refs):
    z, g = accs[0], rows[0]
    for c in range(z.shape[1] // HEAD_DIM):
        sl = slice(c * HEAD_DIM, (c + 1) * HEAD_DIM)
        blk = z[:, sl]
        y = blk * lax.rsqrt(jnp.mean(blk * blk, axis=-1, keepdims=True) + EPS) * g[:, sl]
        for o in out_refs:
            o[:, sl] = y.astype(o.dtype)


def _log_sigmoid(x):
    return -(jnp.maximum(-x, 0.0) + jnp.log1p(jnp.exp(-jnp.abs(x))))


def _epi_logf(accs, haccs, keep, rows, tile_refs, out_refs):
    _store_all(out_refs, _log_sigmoid(accs[0] + rows[0]))


def _epi_sigmoid(accs, haccs, keep, rows, tile_refs, out_refs):
    _store_all(out_refs, jax.nn.sigmoid(accs[0]))


def _causal_conv3(u, prev2, w):
    row = lax.broadcasted_iota(jnp.int32, u.shape, 0)
    u1 = jnp.where(row == 0, prev2[1:2], pltpu.roll(u, 1, 0))
    u2 = jnp.where(row == 0, prev2[0:1], jnp.where(row == 1, prev2[1:2], pltpu.roll(u, 2, 0)))
    y = u2 * w[0:1]
    y = y + u1 * w[1:2]
    return y + u * w[2:3]


def _epi_conv_branch(accs, haccs, keep, rows, tile_refs, out_refs):
    u = accs[1] * accs[2]
    hu = (haccs[1] * haccs[2])[BF16_SUBLANES - 2:BF16_SUBLANES] * keep
    _store_all(out_refs, accs[0] * _causal_conv3(u, hu, rows[0]))


def _epi_ffn_up(accs, haccs, keep, rows, tile_refs, out_refs):
    hg = haccs[0][BF16_SUBLANES - 2:BF16_SUBLANES] * keep
    gc = _causal_conv3(accs[0], hg, rows[0])
    _store_all(out_refs, gc * jax.nn.sigmoid(gc) * accs[1])


def _epi_merge(accs, haccs, keep, rows, tile_refs, out_refs):
    g_a, g_b, g_m = (t[...].astype(F32) for t in tile_refs)
    _store_all(out_refs, g_a * accs[0] + g_b * accs[1] + g_m * accs[2])


def _epi_residual(accs, haccs, keep, rows, tile_refs, out_refs):
    _store_all(out_refs, tile_refs[0][...] + accs[0])


def _cumsum_kernel(x_ref, o_ref, carry_ref, *, rows):
    @pl.when(pl.program_id(1) == 0)
    def _():
        carry_ref[...] = jnp.zeros_like(carry_ref)

    x = x_ref[...]
    row = lax.broadcasted_iota(jnp.int32, x.shape, 0)
    s = 1
    while s < rows:
        x = x + jnp.where(row >= s, pltpu.roll(x, s, 0), 0.0)
        s *= 2
    x = x + carry_ref[0:1, :]
    o_ref[...] = x
    carry_ref[...] = jnp.broadcast_to(x[rows - 1:rows, :], carry_ref.shape)


def seq_cumsum(x, n_seq, *, rows=512):
    m, w = x.shape
    per = m // n_seq // rows
    return pl.pallas_call(
        functools.partial(_cumsum_kernel, rows=rows),
        grid=(n_seq, per),
        in_specs=[pl.BlockSpec((rows, w), lambda b, i: (b * per + i, 0))],
        out_specs=pl.BlockSpec((rows, w), lambda b, i: (b * per + i, 0)),
        out_shape=jax.ShapeDtypeStruct((m, w), F32),
        scratch_shapes=[pltpu.VMEM((8, w), F32)],
        compiler_params=_params(2),
        name="logf_cumsum",
    )(x)


def _token_conv(u, st_ref, w_ref, hist_ref):
    s0, s1 = st_ref[0], st_ref[1]
    w = w_ref[...]
    y = s0 * w[0:1]
    y = y + s1 * w[1:2]
    y = y + u * w[2:3]
    hist_ref[0] = s1
    hist_ref[1] = u
    return y


def _sample_conv_kernel(cb_ref, cc_ref, cx_ref, st_ref, w_ref, b_ref, hist_ref):
    y = _token_conv(cc_ref[...] * cx_ref[...], st_ref, w_ref, hist_ref)
    b_ref[...] = (cb_ref[...] * y).astype(b_ref.dtype)


def _sample_ffn_kernel(g_ref, v_ref, st_ref, w_ref, a_ref, hist_ref):
    y = _token_conv(g_ref[...], st_ref, w_ref, hist_ref)
    a_ref[...] = (y * jax.nn.sigmoid(y) * v_ref[...]).astype(a_ref.dtype)


def token_conv_call(name, body, vecs, state, w, *, tn):
    r, c = vecs[0].shape
    assert c % tn == 0 and tn % LANES == 0
    vec = pl.BlockSpec((r, tn), lambda j: (0, j))
    st = pl.BlockSpec((2, r, tn), lambda j: (0, 0, j))
    return pl.pallas_call(
        body,
        grid=(c // tn,),
        in_specs=[vec] * len(vecs) + [st, pl.BlockSpec((3, tn), lambda j: (0, j))],
        out_specs=[vec, st],
        out_shape=[jax.ShapeDtypeStruct((r, c), BF16), jax.ShapeDtypeStruct((2, r, c), F32)],
        compiler_params=_params(1),
        name=name,
    )(*vecs, state, w)


_NT = (((1,), (1,)), ((), ()))


def _fox_prompt_kernel(q_ref, k_ref, vt_ref, cb_ref, o_ref, s0_ref, s1_ref, smax0_ref, smax1_ref,
                       m_ref, l_ref, acc_ref, *, tq):
    i = pl.program_id(2)
    q = q_ref[...]
    lane_tiles = tq // LANES

    def scores_to(s_ref, smax_ref, j, masked):
        start = pl.multiple_of(j * tq, tq)
        k = k_ref[pl.ds(start, tq), :]
        cb = cb_ref[pl.ds(start, tq), :]
        s = lax.dot_general(k, q, _NT, preferred_element_type=F32) - jnp.concatenate([cb] * lane_tiles, axis=1)
        if masked:
            key = lax.broadcasted_iota(jnp.int32, s.shape, 0)
            qry = lax.broadcasted_iota(jnp.int32, s.shape, 1)
            s = jnp.where(key <= qry, s, -jnp.inf)
        s_ref[...] = s
        smax_ref[...] = jnp.max(s, axis=0, keepdims=True)

    def absorb(s_ref, smax_ref, j):
        start = pl.multiple_of(j * tq, tq)
        m = m_ref[...]
        m_new = jnp.maximum(m, smax_ref[...])
        p = jnp.exp2(s_ref[...] - m_new)
        alpha = jnp.exp2(m - m_new)
        l_ref[...] = alpha * l_ref[...] + jnp.sum(p, axis=0, keepdims=True)
        acc_ref[...] = alpha * acc_ref[...] + jnp.dot(vt_ref[:, pl.ds(start, tq)], p.astype(BF16),
                                                      preferred_element_type=F32)
        m_ref[...] = m_new

    m_ref[...] = jnp.full_like(m_ref, -jnp.inf)
    l_ref[...] = jnp.zeros_like(l_ref)
    acc_ref[...] = jnp.zeros_like(acc_ref)

    scores_to(s0_ref, smax0_ref, i, True)
    n_pairs = i // 2

    def in_s0(u):
        return jnp.where(u == 0, i, 2 * u - 1)

    def pair(u, carry):
        scores_to(s1_ref, smax1_ref, 2 * u, False)
        absorb(s0_ref, smax0_ref, in_s0(u))
        scores_to(s0_ref, smax0_ref, 2 * u + 1, False)
        absorb(s1_ref, smax1_ref, 2 * u)
        return carry

    lax.fori_loop(0, n_pairs, pair, 0)

    @pl.when(i % 2 == 1)
    def _():
        scores_to(s1_ref, smax1_ref, i - 1, False)
        absorb(s0_ref, smax0_ref, in_s0(n_pairs))
        absorb(s1_ref, smax1_ref, i - 1)

    @pl.when(i % 2 == 0)
    def _():
        absorb(s0_ref, smax0_ref, in_s0(n_pairs))

    o_ref[...] = (acc_ref[...] / l_ref[...]).T.astype(o_ref.dtype)


def fox_prompt_attention(q, k, vt, cb, *, batch, seq, tq=512):
    nq = seq // tq
    return pl.pallas_call(
        functools.partial(_fox_prompt_kernel, tq=tq),
        grid=(batch, FOX_HEADS, nq),
        in_specs=[pl.BlockSpec((tq, HEAD_DIM), lambda b, h, i: (b * nq + i, h)),
                  pl.BlockSpec((seq, HEAD_DIM), lambda b, h, i: (b, h)),
                  pl.BlockSpec((HEAD_DIM, seq), lambda b, h, i: (h, b)),
                  pl.BlockSpec((None, seq, LANES), lambda b, h, i: (b * FOX_HEADS + h, 0, 0))],
        out_specs=pl.BlockSpec((tq, HEAD_DIM), lambda b, h, i: (b * nq + i, h)),
        out_shape=jax.ShapeDtypeStruct(q.shape, BF16),
        scratch_shapes=[pltpu.VMEM((tq, tq), F32), pltpu.VMEM((tq, tq), F32),
                        pltpu.VMEM((1, tq), F32), pltpu.VMEM((1, tq), F32),
                        pltpu.VMEM((1, tq), F32), pltpu.VMEM((1, tq), F32), pltpu.VMEM((HEAD_DIM, tq), F32)],
        compiler_params=_params(3),
        name="fox_prompt_attention",
    )(q, k, vt, cb)


def _mem_attention_kernel(q_ref, k_ref, v_ref, o_ref):
    for h in range(MEM_HEADS):
        sl = slice(h * HEAD_DIM, (h + 1) * HEAD_DIM)
        q = q_ref[:, sl]
        k = k_ref[:, sl].astype(BF16)
        v = v_ref[:, sl].astype(BF16)
        s = lax.dot_general(q, k, _NT, preferred_element_type=F32)
        p = jnp.exp(s - jnp.max(s, axis=-1, keepdims=True))
        l = jnp.sum(p, axis=-1, keepdims=True)
        o = jnp.dot(p.astype(BF16), v, preferred_element_type=F32) / l
        o_ref[:, sl] = o.astype(o_ref.dtype)


def mem_attention(q, k, v, *, tq):
    rows = q.shape[0]
    groups = k.shape[0] // MEM_TOKENS
    per = rows // groups // tq
    width = q.shape[1]
    return pl.pallas_call(
        _mem_attention_kernel,
        grid=(groups, per),
        in_specs=[pl.BlockSpec((tq, width), lambda b, i: (b * per + i, 0)),
                  pl.BlockSpec((MEM_TOKENS, width), lambda b, i: (b, 0)),
                  pl.BlockSpec((MEM_TOKENS, width), lambda b, i: (b, 0))],
        out_specs=pl.BlockSpec((tq, width), lambda b, i: (b * per + i, 0)),
        out_shape=jax.ShapeDtypeStruct(q.shape, BF16),
        compiler_params=_params(2),
        name="mem_attention",
    )(q, k, v)


def _mem_attention_sample_kernel(q_ref, k_ref, v_ref, o_ref):
    for h in range(MEM_HEADS):
        sl = slice(h * HEAD_DIM, (h + 1) * HEAD_DIM)
        q = q_ref[0, :, sl]
        k = k_ref[:, sl].astype(BF16)
        v = v_ref[:, sl].astype(BF16)
        s = lax.dot_general(q, k, _NT, preferred_element_type=F32)
        p = jnp.exp(s - jnp.max(s, axis=-1, keepdims=True))
        l = jnp.sum(p, axis=-1, keepdims=True)
        o = jnp.dot(p.astype(BF16), v, preferred_element_type=F32) / l
        o_ref[0, :, sl] = o.astype(o_ref.dtype)


def mem_attention_sample(q, k, v):
    rows, _, width = q.shape
    return pl.pallas_call(
        _mem_attention_sample_kernel,
        grid=(rows,),
        in_specs=[pl.BlockSpec((1, 1, width), lambda b: (b, 0, 0)),
                  pl.BlockSpec((MEM_TOKENS, width), lambda b: (b, 0)),
                  pl.BlockSpec((MEM_TOKENS, width), lambda b: (b, 0))],
        out_specs=pl.BlockSpec((1, 1, width), lambda b: (b, 0, 0)),
        out_shape=jax.ShapeDtypeStruct(q.shape, BF16),
        compiler_params=_params(1),
        name="mem_attention_sample",
    )(q, k, v)


def _fox_decode_kernel(pt_ref, q_ref, kn_ref, vn_ref, fn_ref, u_ref, *refs, n_steps, pp):
    k_refs, v_refs, f_refs = refs[:pp], refs[pp:2 * pp], refs[2 * pp:3 * pp]
    o_ref, m_ref, l_ref, run_ref, acc_ref = refs[3 * pp:]
    j = pl.program_id(1)
    keys = PAGE * FOX_HEADS

    @pl.when(j == 0)
    def _():
        m_ref[...] = jnp.full_like(m_ref, -jnp.inf)
        l_ref[...] = jnp.zeros_like(l_ref)
        run_ref[...] = jnp.zeros_like(run_ref)
        acc_ref[...] = jnp.zeros_like(acc_ref)

    q = q_ref[0]
    head = lax.broadcasted_iota(jnp.int32, (FOX_HEADS, keys), 0)
    lane = lax.broadcasted_iota(jnp.int32, (FOX_HEADS, keys), 1)
    own = (lane % FOX_HEADS) == head

    splits, totals = [], []
    for f_ref in f_refs:
        ft = f_ref[0, 0]
        hi = ft.astype(BF16)
        rest = ft - hi.astype(F32)
        mid = rest.astype(BF16)
        lo = (rest - mid.astype(F32)).astype(BF16)
        splits += [hi, mid, lo]
        totals.append(jnp.sum(ft, axis=-1, keepdims=True))
    e_all = jnp.dot(jnp.concatenate(splits, axis=0), u_ref[...], preferred_element_type=F32)

    run = run_ref[...]
    scores = []
    for i, k_ref in enumerate(k_refs):
        k2 = k_ref[0, 0].reshape(keys, HEAD_DIM).astype(BF16)
        x = lax.dot_general(q, k2, _NT, preferred_element_type=F32)
        r0 = 3 * FOX_HEADS * i
        e = (e_all[r0:r0 + FOX_HEADS] + e_all[r0 + FOX_HEADS:r0 + 2 * FOX_HEADS]
             + e_all[r0 + 2 * FOX_HEADS:r0 + 3 * FOX_HEADS])
        scores.append(jnp.where(own, x + e + run, -jnp.inf))
        run = run + totals[i]
    run_ref[...] = run

    m_old = m_ref[...]
    m_new = m_old
    for s in scores:
        m_new = jnp.maximum(m_new, jnp.max(s, axis=-1, keepdims=True))
    alpha = jnp.exp(m_old - m_new)
    l = alpha * l_ref[...]
    acc = alpha * acc_ref[...]
    for s, v_ref in zip(scores, v_refs):
        p = jnp.exp(s - m_new)
        l = l + jnp.sum(p, axis=-1, keepdims=True)
        v2 = v_ref[0, 0].reshape(keys, HEAD_DIM).astype(BF16)
        acc = acc + jnp.dot(p.astype(BF16), v2, preferred_element_type=F32)
    l_ref[...] = l
    acc_ref[...] = acc
    m_ref[...] = m_new

    @pl.when(j == n_steps - 1)
    def _():
        kn = kn_ref[0].astype(BF16).astype(F32)
        vn = vn_ref[0].astype(BF16).astype(F32)
        s_new = jnp.sum(q.astype(F32) * kn, axis=-1, keepdims=True) - fn_ref[0]
        m_p = m_ref[...]
        m_f = jnp.maximum(m_p, s_new)
        a_f = jnp.exp(m_p - m_f)
        p_new = jnp.exp(s_new - m_f)
        l_f = a_f * l_ref[...] + p_new
        p_new = p_new.astype(BF16).astype(F32)
        o_ref[0] = ((a_f * acc_ref[...] + p_new * vn) / l_f).astype(o_ref.dtype)


def fox_decode_attention(page_table, q, k_new, v_new, logf_new, cache_k, cache_v, cache_logf_t, *, pp=8):
    rows, n_pages = page_table.shape
    keys = PAGE * FOX_HEADS
    assert n_pages % pp == 0
    n_steps = n_pages // pp
    later = (lax.broadcasted_iota(jnp.int32, (PAGE, keys), 0)
             > lax.broadcasted_iota(jnp.int32, (PAGE, keys), 1) // FOX_HEADS).astype(BF16)
    tok = pl.BlockSpec((1, FOX_HEADS, HEAD_DIM), lambda b, j, pt: (b, 0, 0))

    def page_spec(i, page_shape):
        return pl.BlockSpec((1, 1) + page_shape,
                            lambda b, j, pt: (0, pt[b, n_pages - 1 - (j * pp + i)]) + (0,) * len(page_shape))

    grid_spec = pltpu.PrefetchScalarGridSpec(
        num_scalar_prefetch=1,
        grid=(rows, n_steps),
        in_specs=([tok, tok, tok, pl.BlockSpec((1, FOX_HEADS, 1), lambda b, j, pt: (b, 0, 0)),
                   pl.BlockSpec((PAGE, keys), lambda b, j, pt: (0, 0))]
                  + [page_spec(i, (PAGE, FOX_HEADS, HEAD_DIM)) for i in range(pp)] * 2
                  + [page_spec(i, (FOX_HEADS, PAGE)) for i in range(pp)]),
        out_specs=tok,
        scratch_shapes=[pltpu.VMEM((FOX_HEADS, 1), F32), pltpu.VMEM((FOX_HEADS, 1), F32),
                        pltpu.VMEM((FOX_HEADS, 1), F32), pltpu.VMEM((FOX_HEADS, HEAD_DIM), F32)],
    )
    return pl.pallas_call(
        functools.partial(_fox_decode_kernel, n_steps=n_steps, pp=pp),
        grid_spec=grid_spec,
        out_shape=jax.ShapeDtypeStruct((rows, FOX_HEADS, HEAD_DIM), BF16),
        compiler_params=_params(2),
        name="fox_decode_attention",
    )(page_table, q, k_new, v_new, logf_new, later, *([cache_k] * pp), *([cache_v] * pp), *([cache_logf_t] * pp))


def _tile_gain(g, width, scale=1.0):
    return jnp.tile(g.astype(F32) * scale, width // HEAD_DIM).reshape(1, width)


def kernel(x_prompt, mem_prompt, x_sample, cache_k, cache_v, cache_logf, cache_mem_k, cache_mem_v,
           state_conv, state_ffn_conv, page_table, g_mix, w_in, b_f, q_norm_g, k_norm_g, mq_norm_g,
           mk_norm_g, g_mem, w_mem_kv, conv_w, w_o_fox, w_o_conv, w_o_mem, w_o, g_ffn, w_up,
           ffn_conv_w, w_down):
    depth = w_in.shape[0]
    assert depth == 1
    batch, seq, d_model = x_prompt.shape
    dec = x_sample.shape[0]
    assert x_sample.shape[1] == 1
    fox_w = FOX_HEADS * HEAD_DIM
    conv_c = state_conv.shape[-1]
    mem_w = MEM_HEADS * HEAD_DIM
    d_ff = w_down.shape[1]
    tokens = batch * seq
    l = 0

    wt_in = jnp.swapaxes(w_in[l], 0, 1)
    f0 = 3 * fox_w
    n_rest = 3 * conv_c + mem_w
    st = 512
    nb_g, nb_qkv, nb_rest = 3 * d_model // st, f0 // st, n_rest // st
    nb_all = nb_g + nb_qkv + nb_rest
    w_all = cast_transpose_rows(wt_in, [(0, nb_qkv, nb_g), (f0 + FOX_HEADS, nb_rest, nb_g + nb_qkv),
                                        (f0 + FOX_HEADS + n_rest, nb_g, 0), (f0, 1, nb_all)], tn=st)
    o = 0
    def seg(width):
        nonlocal o
        w = (w_all, o, width)
        o += width
        return w
    w_g = seg(3 * d_model)
    w_q, w_k, w_v = seg(fox_w), seg(fox_w), seg(fox_w)
    w_cb, w_cc, w_cx = seg(conv_c), seg(conv_c), seg(conv_c)
    w_mq = seg(mem_w)
    w_f = (w_all, nb_all * st, LANES)
    w_mkv = w_mem_kv[l].astype(BF16)
    w_mk, w_mv = (w_mkv, 0, mem_w), (w_mkv, mem_w, mem_w)
    w_of, w_oc, w_om = ((w[l].astype(BF16), 0, d_model) for w in (w_o_fox, w_o_conv, w_o_mem))
    w_oo = (w_o[l].astype(BF16), 0, d_model)
    w_u = w_up[l].astype(BF16)
    w_ug, w_uv = (w_u, 0, d_ff), (w_u, d_ff, d_ff)
    w_dn = (w_down[l].astype(BF16), 0, d_model)
    b_f_row = jnp.pad(b_f[l].astype(F32), (0, LANES - FOX_HEADS)).reshape(1, LANES)
    g_q2 = _tile_gain(q_norm_g[l], fox_w, ATTN_SCALE * LOG2E)
    g_q = _tile_gain(q_norm_g[l], fox_w, ATTN_SCALE)
    g_k = _tile_gain(k_norm_g[l], fox_w)
    g_mq = _tile_gain(mq_norm_g[l], mem_w, ATTN_SCALE)
    g_mk = _tile_gain(mk_norm_g[l], mem_w)
    cw, fcw = conv_w[l], ffn_conv_w[l]

    def mm(name, acts, ws, epi, **kw):
        return fused_mm(name, acts, [(ai, w[0], w[1]) for ai, w in ws], epi, n=ws[0][1][2], **kw)

    def proj(name, h, w, epi, rows=(), out_dtypes=(F32,), tm=1024, tn=512, **kw):
        return mm(name, [h], [(0, w)], epi, rows=rows, out_dtypes=out_dtypes, tm=tm, tn=tn, **kw)

    xp = x_prompt.reshape(tokens, d_model)
    h = rmsnorm_bf16(xp, g_mix[l])
    (q_p,) = proj("p_q", h, w_q, _epi_headnorm, [g_q2], (BF16,), tn=1024)
    k_p, k_p16 = proj("p_k", h, w_k, _epi_headnorm, [g_k], (F32, BF16))
    v_p, vt_p16 = proj("p_v", h, w_v, _epi_plain_and_transposed, (), (F32, BF16), out_transposed=(False, True))
    (logf_p,) = proj("p_logf", h, w_f, _epi_logf, [b_f_row], (F32,))
    (b_p,) = mm("p_conv", [h], [(0, w_cb), (0, w_cc), (0, w_cx)], _epi_conv_branch,
                rows=[cw], out_dtypes=(BF16,), tm=1024, tn=256, halo_seq=seq)
    (mq_p,) = proj("p_mq", h, w_mq, _epi_headnorm, [g_mq], (BF16,))
    (gates_p,) = proj("p_gates", h, w_g, _epi_sigmoid, (), (BF16,), tn=1024)

    hm = rmsnorm_bf16(mem_prompt.reshape(batch * MEM_TOKENS, d_model), g_mem[l])
    (mk_p,) = proj("p_mk", hm, w_mk, _epi_headnorm, [g_mk], (F32,), tm=512)
    (mv_p,) = proj("p_mv", hm, w_mv, _epi_plain, (), (F32,), tm=512)

    c_p = seq_cumsum(logf_p, batch)[:, :FOX_HEADS] * LOG2E
    c_lanes = jnp.broadcast_to(c_p.reshape(batch, seq, FOX_HEADS).transpose(0, 2, 1)[..., None],
                               (batch, FOX_HEADS, seq, LANES)).reshape(batch * FOX_HEADS, seq, LANES)
    a_p = fox_prompt_attention(q_p, k_p16, vt_p16, c_lanes, batch=batch, seq=seq)
    m_p = mem_attention(mq_p, mk_p, mv_p, tq=1024)

    def merge_out(name, x, a, b, m, gates, tm):
        (merged,) = mm(name + "_merge", [a, b, m], [(0, w_of), (1, w_oc), (2, w_om)], _epi_merge,
                       tiles=[(gates, 0), (gates, d_model), (gates, 2 * d_model)],
                       out_dtypes=(BF16,), tm=tm, tn=512)
        (y,) = mm(name + "_out", [merged], [(0, w_oo)], _epi_residual, tiles=[(x, 0)],
                  out_dtypes=(F32,), tm=tm, tn=512)
        return y

    xp1 = merge_out("p", xp, a_p, b_p, m_p, gates_p, 1024)
    h2 = rmsnorm_bf16(xp1, g_ffn[l])
    (act_p,) = mm("p_ffn_up", [h2], [(0, w_ug), (0, w_uv)], _epi_ffn_up, rows=[fcw],
                  out_dtypes=(BF16,), tm=1024, tn=256, halo_seq=seq)
    (y_p,) = mm("p_ffn_down", [act_p], [(0, w_dn)], _epi_residual, tiles=[(xp1, 0)],
                out_dtypes=(F32,), tm=512, tn=256)

    xs = x_sample.reshape(dec, d_model)
    tail = x_prompt[:, seq - 2:, :].reshape(batch * 2, d_model)
    n_small = dec + batch * 2
    pad_rows = (-n_small) % BF16_SUBLANES
    x_small = jnp.concatenate([xs, tail, jnp.zeros((pad_rows, d_model), F32)], axis=0)
    n_rows = x_small.shape[0]
    hs = rmsnorm_bf16(x_small, g_mix[l])
    sm = dict(tm=n_rows, tn=512)
    (q_s,) = proj("s_q", hs, w_q, _epi_headnorm, [g_q], (BF16,), **sm)
    (k_s,) = proj("s_k", hs, w_k, _epi_headnorm, [g_k], (F32,), **sm)
    (v_s,) = proj("s_v", hs, w_v, _epi_plain, (), (F32,), **sm)
    (logf_s,) = proj("s_logf", hs, w_f, _epi_logf, [b_f_row], (F32,), **sm)
    (cb_s,) = proj("s_cb", hs, w_cb, _epi_plain, (), (F32,), **sm)
    (cc_s,) = proj("s_cc", hs, w_cc, _epi_plain, (), (F32,), **sm)
    (cx_s,) = proj("s_cx", hs, w_cx, _epi_plain, (), (F32,), **sm)
    (mq_s,) = proj("s_mq", hs, w_mq, _epi_headnorm, [g_mq], (BF16,), **sm)
    (gates_s,) = proj("s_gates", hs, w_g, _epi_sigmoid, (), (BF16,), **sm)

    def small_state(state):
        return jnp.pad(state.transpose(1, 0, 2), ((0, 0), (0, n_rows - dec), (0, 0)))

    b_s, conv_hist = token_conv_call("s_conv", _sample_conv_kernel, [cb_s, cc_s, cx_s],
                                     small_state(state_conv[l]), cw, tn=conv_c // 2)
    conv_sample = conv_hist[:, :dec].transpose(1, 0, 2)
    conv_prompt = conv_hist[1, dec:n_small].reshape(batch, 2, conv_c)

    k_new = k_s[:dec].reshape(dec, FOX_HEADS, HEAD_DIM)
    v_new = v_s[:dec].reshape(dec, FOX_HEADS, HEAD_DIM)
    logf_new = logf_s[:dec, :FOX_HEADS]
    a_s = fox_decode_attention(
        page_table, q_s[:dec].reshape(dec, FOX_HEADS, HEAD_DIM), k_new, v_new,
        logf_new.reshape(dec, FOX_HEADS, 1), cache_k[l:l + 1], cache_v[l:l + 1],
        jnp.swapaxes(cache_logf[l:l + 1], 2, 3)).reshape(dec, fox_w)
    m_s = mem_attention_sample(mq_s[:dec].reshape(dec, 1, mem_w),
                               cache_mem_k[l].reshape(dec * MEM_TOKENS, mem_w),
                               cache_mem_v[l].reshape(dec * MEM_TOKENS, mem_w)).reshape(dec, mem_w)
    xs1 = merge_out("s", xs, a_s, b_s[:dec], m_s, gates_s[:dec], dec)

    tail1 = xp1.reshape(batch, seq, d_model)[:, seq - 2:, :].reshape(batch * 2, d_model)
    x_small2 = jnp.concatenate([xs1, tail1, jnp.zeros((pad_rows, d_model), F32)], axis=0)
    hs2 = rmsnorm_bf16(x_small2, g_ffn[l])
    (gate_s,) = proj("s_ffn_gate", hs2, w_ug, _epi_plain, (), (F32,), tm=n_rows, tn=256)
    (val_s,) = proj("s_ffn_val", hs2, w_uv, _epi_plain, (), (F32,), tm=n_rows, tn=256)
    act_s, ffn_hist = token_conv_call("s_ffn_act", _sample_ffn_kernel, [gate_s, val_s],
                                      small_state(state_ffn_conv[l]), fcw, tn=d_ff // 2)
    ffn_sample = ffn_hist[:, :dec].transpose(1, 0, 2)
    ffn_prompt = ffn_hist[1, dec:n_small].reshape(batch, 2, d_ff)
    (y_s,) = mm("s_ffn_down", [act_s[:dec]], [(0, w_dn)], _epi_residual, tiles=[(xs1, 0)],
                out_dtypes=(F32,), tm=dec, tn=256)

    pages = seq // PAGE
    return (y_p.reshape(batch, seq, d_model), y_s.reshape(dec, 1, d_model),
            k_p.reshape(1, batch, pages, PAGE, FOX_HEADS, HEAD_DIM),
            v_p.reshape(1, batch, pages, PAGE, FOX_HEADS, HEAD_DIM),
            logf_p[:, :FOX_HEADS].reshape(1, batch, pages, PAGE, FOX_HEADS),
            mk_p.reshape(1, batch, MEM_TOKENS, MEM_HEADS, HEAD_DIM),
            mv_p.reshape(1, batch, MEM_TOKENS, MEM_HEADS, HEAD_DIM),
            conv_prompt[None], ffn_prompt[None],
            k_new.reshape(1, dec, 1, FOX_HEADS, HEAD_DIM), v_new.reshape(1, dec, 1, FOX_HEADS, HEAD_DIM),
            logf_new.reshape(1, dec, 1, FOX_HEADS), conv_sample[None], ffn_sample[None])
```

```python
import functools

import jax
import jax.numpy as jnp
from jax import lax
from jax.experimental import pallas as pl
from jax.experimental.pallas import tpu as pltpu

F32 = jnp.float32
BF16 = jnp.bfloat16

EPS = 1e-6
HEAD_DIM = 128
LANES = 128
BF16_SUBLANES = 16
FOX_HEADS = 16
MEM_HEADS = 4
MEM_TOKENS = 256
PAGE = 128
ATTN_SCALE = HEAD_DIM ** -0.5
LOG2E = 1.4426950408889634
VMEM_LIMIT = 56 * 1024 * 1024


def _params(n_axes):
    return pltpu.CompilerParams(dimension_semantics=("arbitrary",) * n_axes,
                                vmem_limit_bytes=VMEM_LIMIT)


def _rmsnorm_kernel(x_ref, g_ref, o_ref):
    x = x_ref[...]
    y = x * lax.rsqrt(jnp.mean(x * x, axis=-1, keepdims=True) + EPS)
    o_ref[...] = (y * g_ref[...]).astype(o_ref.dtype)


def rmsnorm_bf16(x, g, *, tr=256):
    m, d = x.shape
    tr = min(tr, m)
    return pl.pallas_call(
        _rmsnorm_kernel,
        grid=(m // tr,),
        in_specs=[pl.BlockSpec((tr, d), lambda i: (i, 0)),
                  pl.BlockSpec((1, d), lambda i: (0, 0))],
        out_specs=pl.BlockSpec((tr, d), lambda i: (i, 0)),
        out_shape=jax.ShapeDtypeStruct((m, d), BF16),
        compiler_params=_params(1),
        name="rmsnorm",
    )(x, g.reshape(1, d))


def _cast_transpose_kernel(x_ref, o_ref):
    o_ref[...] = x_ref[...].astype(o_ref.dtype).T


def cast_transpose_rows(wt, pieces, *, tn=512):
    k = wt.shape[1]
    starts, total = [], 0
    for row0, n_blocks, _ in pieces:
        assert row0 % BF16_SUBLANES == 0 and row0 + n_blocks * tn <= wt.shape[0]
        starts.append(total)
        total += n_blocks

    def pick(j, fn):
        out = 0
        for (row0, n_blocks, dest0), s in zip(pieces, starts):
            out = out + jnp.where((j >= s) & (j < s + n_blocks), fn(row0, dest0, j - s), 0)
        return out

    return pl.pallas_call(
        _cast_transpose_kernel,
        grid=(total,),
        in_specs=[pl.BlockSpec(
            (pl.Element(tn), pl.Element(k)),
            lambda j: (pl.multiple_of(pick(j, lambda row0, dest0, t: row0 + t * tn), BF16_SUBLANES), 0))],
        out_specs=pl.BlockSpec((k, tn), lambda j: (0, pick(j, lambda row0, dest0, t: dest0 + t))),
        out_shape=jax.ShapeDtypeStruct((k, total * tn), BF16),
        compiler_params=_params(1),
        name="cast_transpose_rows",
    )(wt)


def _fused_mm_kernel(*refs, n_act, pair_act, n_rows, n_tiles, n_out, halo, tiles_per_seq, epilogue, stage):
    pos = 0
    act_refs = refs[pos:pos + n_act]; pos += n_act
    halo_ref = None
    if halo:
        halo_ref = refs[pos]; pos += 1
    w_refs = refs[pos:pos + len(pair_act)]; pos += len(pair_act)
    row_refs = refs[pos:pos + n_rows]; pos += n_rows
    tile_refs = refs[pos:pos + n_tiles]; pos += n_tiles
    out_refs = refs[pos:pos + n_out]; pos += n_out
    row_tile = pl.program_id(1 if stage else 0)
    if stage:
        w_f32, w_refs = w_refs, refs[pos:pos + len(pair_act)]

        @pl.when(row_tile == 0)
        def _():
            for src, dst in zip(w_f32, w_refs):
                dst[...] = src[...].astype(dst.dtype)

    accs = [jnp.dot(act_refs[ai][...], w[...], preferred_element_type=F32)
            for ai, w in zip(pair_act, w_refs)]
    haccs, keep = None, None
    if halo:
        haccs = [jnp.dot(halo_ref[...], w[...], preferred_element_type=F32) for w in w_refs]
        keep = jnp.where(row_tile % tiles_per_seq != 0, 1.0, 0.0).astype(F32)
    epilogue(accs, haccs, keep, [r[...] for r in row_refs], tile_refs, out_refs)


def fused_mm(name, acts, pairs, epilogue, *, n, rows=(), tiles=(), out_dtypes, out_transposed=None,
             tm, tn, halo_seq=None, stage_f32_weights=False):
    m = acts[0].shape[0]
    tm = min(tm, m)
    tn = min(tn, n)
    assert m % tm == 0 and n % tn == 0, (name, m, tm, n, tn)
    out_transposed = out_transposed or (False,) * len(out_dtypes)

    def spec(shape, index):
        return pl.BlockSpec(shape, (lambda j, i: index(i, j)) if stage_f32_weights else index)

    in_specs, args = [], []
    for a in acts:
        in_specs.append(spec((tm, a.shape[1]), lambda i, j: (i, 0)))
        args.append(a)
    halo = halo_seq is not None
    if halo:
        assert halo_seq % tm == 0 and tm % BF16_SUBLANES == 0
        step = tm // BF16_SUBLANES
        in_specs.append(spec((BF16_SUBLANES, acts[0].shape[1]),
                             lambda i, j: (jnp.maximum(i * step - 1, 0), 0)))
        args.append(acts[0])
    for _, w, col0 in pairs:
        assert col0 % tn == 0
        in_specs.append(spec((w.shape[0], tn), lambda i, j, off=col0 // tn: (0, j + off)))
        args.append(w)
    for r in rows:
        in_specs.append(spec((r.shape[0], tn), lambda i, j: (0, j)))
        args.append(r)
    for t, col0 in tiles:
        assert col0 % tn == 0
        in_specs.append(spec((tm, tn), lambda i, j, off=col0 // tn: (i, j + off)))
        args.append(t)
    kern = functools.partial(
        _fused_mm_kernel, n_act=len(acts), pair_act=tuple(p[0] for p in pairs), n_rows=len(rows),
        n_tiles=len(tiles), n_out=len(out_dtypes), halo=halo,
        tiles_per_seq=(halo_seq // tm if halo else 1), epilogue=epilogue, stage=stage_f32_weights)
    return pl.pallas_call(
        kern,
        grid=(n // tn, m // tm) if stage_f32_weights else (m // tm, n // tn),
        in_specs=in_specs,
        out_specs=[spec((tn, tm), lambda i, j: (j, i)) if tr else spec((tm, tn), lambda i, j: (i, j))
                   for tr in out_transposed],
        out_shape=[jax.ShapeDtypeStruct((n, m) if tr else (m, n), dt)
                   for dt, tr in zip(out_dtypes, out_transposed)],
        scratch_shapes=([pltpu.VMEM((w.shape[0], tn), BF16) for _, w, _ in pairs] if stage_f32_weights else []),
        compiler_params=_params(2),
        name=name,
    )(*args)


def _store_all(out_refs, y):
    for o in out_refs:
        o[...] = y.astype(o.dtype)


def _epi_plain(accs, haccs, keep, rows, tile_refs, out_refs):
    _store_all(out_refs, accs[0])


def _epi_each(accs, haccs, keep, rows, tile_refs, out_refs):
    for o, acc in zip(out_refs, accs):
        o[...] = acc.astype(o.dtype)


def _epi_plain_and_transposed(accs, haccs, keep, rows, tile_refs, out_refs):
    out_refs[0][...] = accs[0]
    out_refs[1][...] = accs[0].astype(out_refs[1].dtype).T


def _epi_headnorm(accs, haccs, keep, rows, tile_refs, out_refs):
    z, g = accs[0], rows[0]
    for c in range(z.shape[1] // HEAD_DIM):
        sl = slice(c * HEAD_DIM, (c + 1) * HEAD_DIM)
        blk = z[:, sl]
        y = blk * lax.rsqrt(jnp.mean(blk * blk, axis=-1, keepdims=True) + EPS) * g[:, sl]
        for o in out_refs:
            o[:, sl] = y.astype(o.dtype)


def _log_sigmoid(x):
    return -(jnp.maximum(-x, 0.0) + jnp.log1p(jnp.exp(-jnp.abs(x))))


def _epi_logf(accs, haccs, keep, rows, tile_refs, out_refs):
    _store_all(out_refs, _log_sigmoid(accs[0] + rows[0]))


def _epi_sigmoid(accs, haccs, keep, rows, tile_refs, out_refs):
    _store_all(out_refs, jax.nn.sigmoid(accs[0]))


def _causal_conv3(u, prev2, w):
    row = lax.broadcasted_iota(jnp.int32, u.shape, 0)
    u1 = jnp.where(row == 0, prev2[1:2], pltpu.roll(u, 1, 0))
    u2 = jnp.where(row == 0, prev2[0:1], jnp.where(row == 1, prev2[1:2], pltpu.roll(u, 2, 0)))
    y = u2 * w[0:1]
    y = y + u1 * w[1:2]
    return y + u * w[2:3]


def _epi_conv_branch(accs, haccs, keep, rows, tile_refs, out_refs):
    u = accs[1] * accs[2]
    hu = (haccs[1] * haccs[2])[BF16_SUBLANES - 2:BF16_SUBLANES] * keep
    _store_all(out_refs, accs[0] * _causal_conv3(u, hu, rows[0]))


def _epi_ffn_up(accs, haccs, keep, rows, tile_refs, out_refs):
    hg = haccs[0][BF16_SUBLANES - 2:BF16_SUBLANES] * keep
    gc = _causal_conv3(accs[0], hg, rows[0])
    _store_all(out_refs, gc * jax.nn.sigmoid(gc) * accs[1])


def _epi_merge(accs, haccs, keep, rows, tile_refs, out_refs):
    g_a, g_b, g_m = (t[...].astype(F32) for t in tile_refs)
    _store_all(out_refs, g_a * accs[0] + g_b * accs[1] + g_m * accs[2])


def _epi_residual(accs, haccs, keep, rows, tile_refs, out_refs):
    _store_all(out_refs, tile_refs[0][...] + accs[0])


def _cumsum_kernel(x_ref, o_ref, carry_ref, *, rows):
    @pl.when(pl.program_id(1) == 0)
    def _():
        carry_ref[...] = jnp.zeros_like(carry_ref)

    x = x_ref[...]
    row = lax.broadcasted_iota(jnp.int32, x.shape, 0)
    s = 1
    while s < rows:
        x = x + jnp.where(row >= s, pltpu.roll(x, s, 0), 0.0)
        s *= 2
    x = x + carry_ref[0:1, :]
    o_ref[...] = x
    carry_ref[...] = jnp.broadcast_to(x[rows - 1:rows, :], carry_ref.shape)


def seq_cumsum(x, n_seq, *, rows=512):
    m, w = x.shape
    per = m // n_seq // rows
    return pl.pallas_call(
        functools.partial(_cumsum_kernel, rows=rows),
        grid=(n_seq, per),
        in_specs=[pl.BlockSpec((rows, w), lambda b, i: (b * per + i, 0))],
        out_specs=pl.BlockSpec((rows, w), lambda b, i: (b * per + i, 0)),
        out_shape=jax.ShapeDtypeStruct((m, w), F32),
        scratch_shapes=[pltpu.VMEM((8, w), F32)],
        compiler_params=_params(2),
        name="logf_cumsum",
    )(x)


def _token_conv(u, st_ref, w_ref, hist_ref):
    s0, s1 = st_ref[0], st_ref[1]
    w = w_ref[...]
    y = s0 * w[0:1]
    y = y + s1 * w[1:2]
    y = y + u * w[2:3]
    hist_ref[0] = s1
    hist_ref[1] = u
    return y


def _sample_conv_kernel(cb_ref, cc_ref, cx_ref, st_ref, w_ref, b_ref, hist_ref):
    y = _token_conv(cc_ref[...] * cx_ref[...], st_ref, w_ref, hist_ref)
    b_ref[...] = (cb_ref[...] * y).astype(b_ref.dtype)


def _sample_ffn_kernel(g_ref, v_ref, st_ref, w_ref, a_ref, hist_ref):
    y = _token_conv(g_ref[...], st_ref, w_ref, hist_ref)
    a_ref[...] = (y * jax.nn.sigmoid(y) * v_ref[...]).astype(a_ref.dtype)


def token_conv_call(name, body, vecs, state, w, *, tn):
    r, c = vecs[0].shape
    assert c % tn == 0 and tn % LANES == 0
    vec = pl.BlockSpec((r, tn), lambda j: (0, j))
    st = pl.BlockSpec((2, r, tn), lambda j: (0, 0, j))
    return pl.pallas_call(
        body,
        grid=(c // tn,),
        in_specs=[vec] * len(vecs) + [st, pl.BlockSpec((3, tn), lambda j: (0, j))],
        out_specs=[vec, st],
        out_shape=[jax.ShapeDtypeStruct((r, c), BF16), jax.ShapeDtypeStruct((2, r, c), F32)],
        compiler_params=_params(1),
        name=name,
    )(*vecs, state, w)


_NT = (((1,), (1,)), ((), ()))


def _fox_prompt_kernel(q_ref, k_ref, vt_ref, cb_ref, o_ref, s0_ref, s1_ref, smax0_ref, smax1_ref,
                       m_ref, l_ref, acc_ref, *, tq):
    i = pl.program_id(2)
    q = q_ref[...]
    lane_tiles = tq // LANES

    def scores_to(s_ref, smax_ref, j, masked):
        start = pl.multiple_of(j * tq, tq)
        k = k_ref[pl.ds(start, tq), :]
        cb = cb_ref[pl.ds(start, tq), :]
        s = lax.dot_general(k, q, _NT, preferred_element_type=F32) - jnp.concatenate([cb] * lane_tiles, axis=1)
        if masked:
            key = lax.broadcasted_iota(jnp.int32, s.shape, 0)
            qry = lax.broadcasted_iota(jnp.int32, s.shape, 1)
            s = jnp.where(key <= qry, s, -jnp.inf)
        s_ref[...] = s
        smax_ref[...] = jnp.max(s, axis=0, keepdims=True)

    def absorb(s_ref, smax_ref, j):
        start = pl.multiple_of(j * tq, tq)
        m = m_ref[...]
        m_new = jnp.maximum(m, smax_ref[...])
        p = jnp.exp2(s_ref[...] - m_new)
        alpha = jnp.exp2(m - m_new)
        l_ref[...] = alpha * l_ref[...] + jnp.sum(p, axis=0, keepdims=True)
        acc_ref[...] = alpha * acc_ref[...] + jnp.dot(vt_ref[:, pl.ds(start, tq)], p.astype(BF16),
                                                      preferred_element_type=F32)
        m_ref[...] = m_new

    m_ref[...] = jnp.full_like(m_ref, -jnp.inf)
    l_ref[...] = jnp.zeros_like(l_ref)
    acc_ref[...] = jnp.zeros_like(acc_ref)

    scores_to(s0_ref, smax0_ref, i, True)
    n_pairs = i // 2

    def in_s0(u):
        return jnp.where(u == 0, i, 2 * u - 1)

    def pair(u, carry):
        scores_to(s1_ref, smax1_ref, 2 * u, False)
        absorb(s0_ref, smax0_ref, in_s0(u))
        scores_to(s0_ref, smax0_ref, 2 * u + 1, False)
        absorb(s1_ref, smax1_ref, 2 * u)
        return carry

    lax.fori_loop(0, n_pairs, pair, 0)

    @pl.when(i % 2 == 1)
    def _():
        scores_to(s1_ref, smax1_ref, i - 1, False)
        absorb(s0_ref, smax0_ref, in_s0(n_pairs))
        absorb(s1_ref, smax1_ref, i - 1)

    @pl.when(i % 2 == 0)
    def _():
        absorb(s0_ref, smax0_ref, in_s0(n_pairs))

    o_ref[...] = (acc_ref[...] / l_ref[...]).T.astype(o_ref.dtype)


def fox_prompt_attention(q, k, vt, cb, *, batch, seq, tq=512):
    nq = seq // tq
    return pl.pallas_call(
        functools.partial(_fox_prompt_kernel, tq=tq),
        grid=(batch, FOX_HEADS, nq),
        in_specs=[pl.BlockSpec((tq, HEAD_DIM), lambda b, h, i: (b * nq + i, h)),
                  pl.BlockSpec((seq, HEAD_DIM), lambda b, h, i: (b, h)),
                  pl.BlockSpec((HEAD_DIM, seq), lambda b, h, i: (h, b)),
                  pl.BlockSpec((None, seq, LANES), lambda b, h, i: (b * FOX_HEADS + h, 0, 0))],
        out_specs=pl.BlockSpec((tq, HEAD_DIM), lambda b, h, i: (b * nq + i, h)),
        out_shape=jax.ShapeDtypeStruct(q.shape, BF16),
        scratch_shapes=[pltpu.VMEM((tq, tq), F32), pltpu.VMEM((tq, tq), F32),
                        pltpu.VMEM((1, tq), F32), pltpu.VMEM((1, tq), F32),
                        pltpu.VMEM((1, tq), F32), pltpu.VMEM((1, tq), F32), pltpu.VMEM((HEAD_DIM, tq), F32)],
        compiler_params=_params(3),
        name="fox_prompt_attention",
    )(q, k, vt, cb)


def _mem_attention_kernel(q_ref, k_ref, v_ref, o_ref):
    for h in range(MEM_HEADS):
        sl = slice(h * HEAD_DIM, (h + 1) * HEAD_DIM)
        q = q_ref[:, sl]
        k = k_ref[:, sl].astype(BF16)
        v = v_ref[:, sl].astype(BF16)
        s = lax.dot_general(q, k, _NT, preferred_element_type=F32)
        p = jnp.exp(s - jnp.max(s, axis=-1, keepdims=True))
        l = jnp.sum(p, axis=-1, keepdims=True)
        o = jnp.dot(p.astype(BF16), v, preferred_element_type=F32) / l
        o_ref[:, sl] = o.astype(o_ref.dtype)


def mem_attention(q, k, v, *, tq):
    rows = q.shape[0]
    groups = k.shape[0] // MEM_TOKENS
    per = rows // groups // tq
    width = q.shape[1]
    return pl.pallas_call(
        _mem_attention_kernel,
        grid=(groups, per),
        in_specs=[pl.BlockSpec((tq, width), lambda b, i: (b * per + i, 0)),
                  pl.BlockSpec((MEM_TOKENS, width), lambda b, i: (b, 0)),
                  pl.BlockSpec((MEM_TOKENS, width), lambda b, i: (b, 0))],
        out_specs=pl.BlockSpec((tq, width), lambda b, i: (b * per + i, 0)),
        out_shape=jax.ShapeDtypeStruct(q.shape, BF16),
        compiler_params=_params(2),
        name="mem_attention",
    )(q, k, v)


def _mem_attention_sample_kernel(q_ref, k_ref, v_ref, o_ref):
    for h in range(MEM_HEADS):
        sl = slice(h * HEAD_DIM, (h + 1) * HEAD_DIM)
        q = q_ref[0, :, sl]
        k = k_ref[:, sl].astype(BF16)
        v = v_ref[:, sl].astype(BF16)
        s = lax.dot_general(q, k, _NT, preferred_element_type=F32)
        p = jnp.exp(s - jnp.max(s, axis=-1, keepdims=True))
        l = jnp.sum(p, axis=-1, keepdims=True)
        o = jnp.dot(p.astype(BF16), v, preferred_element_type=F32) / l
        o_ref[0, :, sl] = o.astype(o_ref.dtype)


def mem_attention_sample(q, k, v):
    rows, _, width = q.shape
    return pl.pallas_call(
        _mem_attention_sample_kernel,
        grid=(rows,),
        in_specs=[pl.BlockSpec((1, 1, width), lambda b: (b, 0, 0)),
                  pl.BlockSpec((MEM_TOKENS, width), lambda b: (b, 0)),
                  pl.BlockSpec((MEM_TOKENS, width), lambda b: (b, 0))],
        out_specs=pl.BlockSpec((1, 1, width), lambda b: (b, 0, 0)),
        out_shape=jax.ShapeDtypeStruct(q.shape, BF16),
        compiler_params=_params(1),
        name="mem_attention_sample",
    )(q, k, v)


def _fox_decode_kernel(pt_ref, q_ref, kn_ref, vn_ref, fn_ref, u_ref, *refs, n_steps, pp):
    k_refs, v_refs, f_refs = refs[:pp], refs[pp:2 * pp], refs[2 * pp:3 * pp]
    o_ref, m_ref, l_ref, run_ref, acc_ref = refs[3 * pp:]
    j = pl.program_id(1)
    keys = PAGE * FOX_HEADS

    @pl.when(j == 0)
    def _():
        m_ref[...] = jnp.full_like(m_ref, -jnp.inf)
        l_ref[...] = jnp.zeros_like(l_ref)
        run_ref[...] = jnp.zeros_like(run_ref)
        acc_ref[...] = jnp.zeros_like(acc_ref)

    q = q_ref[0]
    head = lax.broadcasted_iota(jnp.int32, (FOX_HEADS, keys), 0)
    lane = lax.broadcasted_iota(jnp.int32, (FOX_HEADS, keys), 1)
    own = (lane % FOX_HEADS) == head

    splits, totals = [], []
    for f_ref in f_refs:
        ft = f_ref[0, 0]
        hi = ft.astype(BF16)
        rest = ft - hi.astype(F32)
        mid = rest.astype(BF16)
        lo = (rest - mid.astype(F32)).astype(BF16)
        splits += [hi, mid, lo]
        totals.append(jnp.sum(ft, axis=-1, keepdims=True))
    e_all = jnp.dot(jnp.concatenate(splits, axis=0), u_ref[...], preferred_element_type=F32)

    run = run_ref[...]
    scores = []
    for i, k_ref in enumerate(k_refs):
        k2 = k_ref[0, 0].reshape(keys, HEAD_DIM).astype(BF16)
        x = lax.dot_general(q, k2, _NT, preferred_element_type=F32)
        r0 = 3 * FOX_HEADS * i
        e = (e_all[r0:r0 + FOX_HEADS] + e_all[r0 + FOX_HEADS:r0 + 2 * FOX_HEADS]
             + e_all[r0 + 2 * FOX_HEADS:r0 + 3 * FOX_HEADS])
        scores.append(jnp.where(own, x + e + run, -jnp.inf))
        run = run + totals[i]
    run_ref[...] = run

    m_old = m_ref[...]
    m_new = m_old
    for s in scores:
        m_new = jnp.maximum(m_new, jnp.max(s, axis=-1, keepdims=True))
    alpha = jnp.exp(m_old - m_new)
    l = alpha * l_ref[...]
    acc = alpha * acc_ref[...]
    for s, v_ref in zip(scores, v_refs):
        p = jnp.exp(s - m_new)
        l = l + jnp.sum(p, axis=-1, keepdims=True)
        v2 = v_ref[0, 0].reshape(keys, HEAD_DIM).astype(BF16)
        acc = acc + jnp.dot(p.astype(BF16), v2, preferred_element_type=F32)
    l_ref[...] = l
    acc_ref[...] = acc
    m_ref[...] = m_new

    @pl.when(j == n_steps - 1)
    def _():
        kn = kn_ref[0].astype(BF16).astype(F32)
        vn = vn_ref[0].astype(BF16).astype(F32)
        s_new = jnp.sum(q.astype(F32) * kn, axis=-1, keepdims=True) - fn_ref[0]
        m_p = m_ref[...]
        m_f = jnp.maximum(m_p, s_new)
        a_f = jnp.exp(m_p - m_f)
        p_new = jnp.exp(s_new - m_f)
        l_f = a_f * l_ref[...] + p_new
        p_new = p_new.astype(BF16).astype(F32)
        o_ref[0] = ((a_f * acc_ref[...] + p_new * vn) / l_f).astype(o_ref.dtype)


def fox_decode_attention(page_table, q, k_new, v_new, logf_new, cache_k, cache_v, cache_logf_t, *, pp=8):
    rows, n_pages = page_table.shape
    keys = PAGE * FOX_HEADS
    assert n_pages % pp == 0
    n_steps = n_pages // pp
    later = (lax.broadcasted_iota(jnp.int32, (PAGE, keys), 0)
             > lax.broadcasted_iota(jnp.int32, (PAGE, keys), 1) // FOX_HEADS).astype(BF16)
    tok = pl.BlockSpec((1, FOX_HEADS, HEAD_DIM), lambda b, j, pt: (b, 0, 0))

    def page_spec(i, page_shape):
        return pl.BlockSpec((1, 1) + page_shape,
                            lambda b, j, pt: (0, pt[b, n_pages - 1 - (j * pp + i)]) + (0,) * len(page_shape))

    grid_spec = pltpu.PrefetchScalarGridSpec(
        num_scalar_prefetch=1,
        grid=(rows, n_steps),
        in_specs=([tok, tok, tok, pl.BlockSpec((1, FOX_HEADS, 1), lambda b, j, pt: (b, 0, 0)),
                   pl.BlockSpec((PAGE, keys), lambda b, j, pt: (0, 0))]
                  + [page_spec(i, (PAGE, FOX_HEADS, HEAD_DIM)) for i in range(pp)] * 2
                  + [page_spec(i, (FOX_HEADS, PAGE)) for i in range(pp)]),
        out_specs=tok,
        scratch_shapes=[pltpu.VMEM((FOX_HEADS, 1), F32), pltpu.VMEM((FOX_HEADS, 1), F32),
                        pltpu.VMEM((FOX_HEADS, 1), F32), pltpu.VMEM((FOX_HEADS, HEAD_DIM), F32)],
    )
    return pl.pallas_call(
        functools.partial(_fox_decode_kernel, n_steps=n_steps, pp=pp),
        grid_spec=grid_spec,
        out_shape=jax.ShapeDtypeStruct((rows, FOX_HEADS, HEAD_DIM), BF16),
        compiler_params=_params(2),
        name="fox_decode_attention",
    )(page_table, q, k_new, v_new, logf_new, later, *([cache_k] * pp), *([cache_v] * pp), *([cache_logf_t] * pp))


def _tile_gain(g, width, scale=1.0):
    return jnp.tile(g.astype(F32) * scale, width // HEAD_DIM).reshape(1, width)


def kernel(x_prompt, mem_prompt, x_sample, cache_k, cache_v, cache_logf, cache_mem_k, cache_mem_v,
           state_conv, state_ffn_conv, page_table, g_mix, w_in, b_f, q_norm_g, k_norm_g, mq_norm_g,
           mk_norm_g, g_mem, w_mem_kv, conv_w, w_o_fox, w_o_conv, w_o_mem, w_o, g_ffn, w_up,
           ffn_conv_w, w_down):
    depth = w_in.shape[0]
    assert depth == 1
    batch, seq, d_model = x_prompt.shape
    dec = x_sample.shape[0]
    assert x_sample.shape[1] == 1
    fox_w = FOX_HEADS * HEAD_DIM
    conv_c = state_conv.shape[-1]
    mem_w = MEM_HEADS * HEAD_DIM
    d_ff = w_down.shape[1]
    tokens = batch * seq
    l = 0

    wt_in = jnp.swapaxes(w_in[l], 0, 1)
    f0 = 3 * fox_w
    n_rest = 3 * conv_c + mem_w
    st = 512
    nb_g, nb_qkv, nb_rest = 3 * d_model // st, f0 // st, n_rest // st
    nb_all = nb_g + nb_qkv + nb_rest
    w_all = cast_transpose_rows(wt_in, [(0, nb_qkv, nb_g), (f0 + FOX_HEADS, nb_rest, nb_g + nb_qkv),
                                        (f0 + FOX_HEADS + n_rest, nb_g, 0), (f0, 1, nb_all)], tn=st)
    o = 0
    def seg(width):
        nonlocal o
        w = (w_all, o, width)
        o += width
        return w
    w_g = seg(3 * d_model)
    w_q, w_k, w_v = seg(fox_w), seg(fox_w), seg(fox_w)
    w_cb, w_cc, w_cx = seg(conv_c), seg(conv_c), seg(conv_c)
    w_mq = seg(mem_w)
    w_f = (w_all, nb_all * st, LANES)
    w_mkv = w_mem_kv[l].astype(BF16)
    w_mk, w_mv = (w_mkv, 0, mem_w), (w_mkv, mem_w, mem_w)
    w_of, w_oc, w_om = ((w[l].astype(BF16), 0, d_model) for w in (w_o_fox, w_o_conv, w_o_mem))
    w_oo = (w_o[l].astype(BF16), 0, d_model)
    w_ug, w_uv = (w_up[l], 0, d_ff), (w_up[l], d_ff, d_ff)
    w_dn = (w_down[l].astype(BF16), 0, d_model)
    b_f_row = jnp.pad(b_f[l].astype(F32), (0, LANES - FOX_HEADS)).reshape(1, LANES)
    g_q2 = _tile_gain(q_norm_g[l], fox_w, ATTN_SCALE * LOG2E)
    g_q = _tile_gain(q_norm_g[l], fox_w, ATTN_SCALE)
    g_k = _tile_gain(k_norm_g[l], fox_w)
    g_mq = _tile_gain(mq_norm_g[l], mem_w, ATTN_SCALE)
    g_mk = _tile_gain(mk_norm_g[l], mem_w)
    cw, fcw = conv_w[l], ffn_conv_w[l]

    def mm(name, acts, ws, epi, **kw):
        return fused_mm(name, acts, [(ai, w[0], w[1]) for ai, w in ws], epi, n=ws[0][1][2], **kw)

    def proj(name, h, w, epi, rows=(), out_dtypes=(F32,), tm=1024, tn=512, **kw):
        return mm(name, [h], [(0, w)], epi, rows=rows, out_dtypes=out_dtypes, tm=tm, tn=tn, **kw)

    xp = x_prompt.reshape(tokens, d_model)
    h = rmsnorm_bf16(xp, g_mix[l])
    (q_p,) = proj("p_q", h, w_q, _epi_headnorm, [g_q2], (BF16,), tn=1024)
    k_p, k_p16 = proj("p_k", h, w_k, _epi_headnorm, [g_k], (F32, BF16))
    v_p, vt_p16 = proj("p_v", h, w_v, _epi_plain_and_transposed, (), (F32, BF16), out_transposed=(False, True))
    (logf_p,) = proj("p_logf", h, w_f, _epi_logf, [b_f_row], (F32,))
    (b_p,) = mm("p_conv", [h], [(0, w_cb), (0, w_cc), (0, w_cx)], _epi_conv_branch,
                rows=[cw], out_dtypes=(BF16,), tm=1024, tn=256, halo_seq=seq)
    (mq_p,) = proj("p_mq", h, w_mq, _epi_headnorm, [g_mq], (BF16,))
    (gates_p,) = proj("p_gates", h, w_g, _epi_sigmoid, (), (BF16,), tn=1024)

    hm = rmsnorm_bf16(mem_prompt.reshape(batch * MEM_TOKENS, d_model), g_mem[l])
    (mk_p,) = proj("p_mk", hm, w_mk, _epi_headnorm, [g_mk], (F32,), tm=512)
    (mv_p,) = proj("p_mv", hm, w_mv, _epi_plain, (), (F32,), tm=512)

    c_p = seq_cumsum(logf_p, batch)[:, :FOX_HEADS] * LOG2E
    c_lanes = jnp.broadcast_to(c_p.reshape(batch, seq, FOX_HEADS).transpose(0, 2, 1)[..., None],
                               (batch, FOX_HEADS, seq, LANES)).reshape(batch * FOX_HEADS, seq, LANES)
    a_p = fox_prompt_attention(q_p, k_p16, vt_p16, c_lanes, batch=batch, seq=seq)
    m_p = mem_attention(mq_p, mk_p, mv_p, tq=1024)

    def merge_out(name, x, a, b, m, gates, tm):
        (merged,) = mm(name + "_merge", [a, b, m], [(0, w_of), (1, w_oc), (2, w_om)], _epi_merge,
                       tiles=[(gates, 0), (gates, d_model), (gates, 2 * d_model)],
                       out_dtypes=(BF16,), tm=tm, tn=512)
        (y,) = mm(name + "_out", [merged], [(0, w_oo)], _epi_residual, tiles=[(x, 0)],
                  out_dtypes=(F32,), tm=tm, tn=512)
        return y

    xp1 = merge_out("p", xp, a_p, b_p, m_p, gates_p, 1024)
    h2 = rmsnorm_bf16(xp1, g_ffn[l])
    (act_p,) = mm("p_ffn_up", [h2], [(0, w_ug), (0, w_uv)], _epi_ffn_up, rows=[fcw],
                  out_dtypes=(BF16,), tm=1024, tn=256, halo_seq=seq, stage_f32_weights=True)
    (y_p,) = mm("p_ffn_down", [act_p], [(0, w_dn)], _epi_residual, tiles=[(xp1, 0)],
                out_dtypes=(F32,), tm=512, tn=256)

    xs = x_sample.reshape(dec, d_model)
    tail = x_prompt[:, seq - 2:, :].reshape(batch * 2, d_model)
    n_small = dec + batch * 2
    pad_rows = (-n_small) % BF16_SUBLANES
    x_small = jnp.concatenate([xs, tail, jnp.zeros((pad_rows, d_model), F32)], axis=0)
    n_rows = x_small.shape[0]
    hs = rmsnorm_bf16(x_small, g_mix[l])
    sm = dict(tm=n_rows, tn=512)
    (q_s,) = proj("s_q", hs, w_q, _epi_headnorm, [g_q], (BF16,), **sm)
    (k_s,) = proj("s_k", hs, w_k, _epi_headnorm, [g_k], (F32,), **sm)
    (v_s,) = proj("s_v", hs, w_v, _epi_plain, (), (F32,), **sm)
    (logf_s,) = proj("s_logf", hs, w_f, _epi_logf, [b_f_row], (F32,), **sm)
    (cb_s,) = proj("s_cb", hs, w_cb, _epi_plain, (), (F32,), **sm)
    (cc_s,) = proj("s_cc", hs, w_cc, _epi_plain, (), (F32,), **sm)
    (cx_s,) = proj("s_cx", hs, w_cx, _epi_plain, (), (F32,), **sm)
    (mq_s,) = proj("s_mq", hs, w_mq, _epi_headnorm, [g_mq], (BF16,), **sm)
    (gates_s,) = proj("s_gates", hs, w_g, _epi_sigmoid, (), (BF16,), **sm)

    def small_state(state):
        return jnp.pad(state.transpose(1, 0, 2), ((0, 0), (0, n_rows - dec), (0, 0)))

    b_s, conv_hist = token_conv_call("s_conv", _sample_conv_kernel, [cb_s, cc_s, cx_s],
                                     small_state(state_conv[l]), cw, tn=conv_c // 2)
    conv_sample = conv_hist[:, :dec].transpose(1, 0, 2)
    conv_prompt = conv_hist[1, dec:n_small].reshape(batch, 2, conv_c)

    k_new = k_s[:dec].reshape(dec, FOX_HEADS, HEAD_DIM)
    v_new = v_s[:dec].reshape(dec, FOX_HEADS, HEAD_DIM)
    logf_new = logf_s[:dec, :FOX_HEADS]
    a_s = fox_decode_attention(
        page_table, q_s[:dec].reshape(dec, FOX_HEADS, HEAD_DIM), k_new, v_new,
        logf_new.reshape(dec, FOX_HEADS, 1), cache_k[l:l + 1], cache_v[l:l + 1],
        jnp.swapaxes(cache_logf[l:l + 1], 2, 3)).reshape(dec, fox_w)
    m_s = mem_attention_sample(mq_s[:dec].reshape(dec, 1, mem_w),
                               cache_mem_k[l].reshape(dec * MEM_TOKENS, mem_w),
                               cache_mem_v[l].reshape(dec * MEM_TOKENS, mem_w)).reshape(dec, mem_w)
    xs1 = merge_out("s", xs, a_s, b_s[:dec], m_s, gates_s[:dec], dec)

    tail1 = xp1.reshape(batch, seq, d_model)[:, seq - 2:, :].reshape(batch * 2, d_model)
    x_small2 = jnp.concatenate([xs1, tail1, jnp.zeros((pad_rows, d_model), F32)], axis=0)
    hs2 = rmsnorm_bf16(x_small2, g_ffn[l])
    gate_s, val_s = mm("s_ffn_up", [hs2], [(0, w_ug), (0, w_uv)], _epi_each, out_dtypes=(F32, F32),
                       tm=n_rows, tn=256, stage_f32_weights=True)
    act_s, ffn_hist = token_conv_call("s_ffn_act", _sample_ffn_kernel, [gate_s, val_s],
                                      small_state(state_ffn_conv[l]), fcw, tn=d_ff // 2)
    ffn_sample = ffn_hist[:, :dec].transpose(1, 0, 2)
    ffn_prompt = ffn_hist[1, dec:n_small].reshape(batch, 2, d_ff)
    (y_s,) = mm("s_ffn_down", [act_s[:dec]], [(0, w_dn)], _epi_residual, tiles=[(xs1, 0)],
                out_dtypes=(F32,), tm=dec, tn=256)

    pages = seq // PAGE
    return (y_p.reshape(batch, seq, d_model), y_s.reshape(dec, 1, d_model),
            k_p.reshape(1, batch, pages, PAGE, FOX_HEADS, HEAD_DIM),
            v_p.reshape(1, batch, pages, PAGE, FOX_HEADS, HEAD_DIM),
            logf_p[:, :FOX_HEADS].reshape(1, batch, pages, PAGE, FOX_HEADS),
            mk_p.reshape(1, batch, MEM_TOKENS, MEM_HEADS, HEAD_DIM),
            mv_p.reshape(1, batch, MEM_TOKENS, MEM_HEADS, HEAD_DIM),
            conv_prompt[None], ffn_prompt[None],
            k_new.reshape(1, dec, 1, FOX_HEADS, HEAD_DIM), v_new.reshape(1, dec, 1, FOX_HEADS, HEAD_DIM),
            logf_new.reshape(1, dec, 1, FOX_HEADS), conv_sample[None], ffn_sample[None])
```
